```python
import jax, jax.numpy as jnp
from jax import lax
import numpy as np

D_MODEL = 1024
BATCH = 8
SEQ = 4096
DEPTH = 1

N_MEM = 256
ML_HEADS = 4
ML_DIM = 1024
ML_HEAD_DIM = ML_DIM // ML_HEADS
ML_CHUNK = 128
HG_EXPAND = 128
HG_DIM = 1024
HG_HEADS = HG_DIM // HG_EXPAND
HG_VDIM = HG_DIM // HG_HEADS
HG_CHUNK = 64
X_HEADS = 4
X_HEAD_DIM = D_MODEL // X_HEADS
D_FF = 2816
CONV_W = 3
EPS = 1e-6
IN_SPLITS = (2 * ML_DIM, ML_DIM, ML_DIM, 4 * ML_HEADS,
             HG_DIM, HG_DIM, HG_DIM, HG_DIM, HG_DIM,
             D_MODEL, D_MODEL)
IN_DIM = 4 * ML_DIM + 4 * ML_HEADS + 5 * HG_DIM + 2 * D_MODEL

kernel_name = "hybrid_mlstm_hgrn2_gated_merge_encoder"


def rmsnorm(x, g):
    xf = x.astype(jnp.float32)
    y = xf * lax.rsqrt(jnp.mean(xf * xf, axis=-1, keepdims=True) + EPS)
    return (y * g.astype(jnp.float32)).astype(x.dtype)


def dwconv(x, w, b):
    c = x.shape[-1]
    k = w.shape[0]
    y = lax.conv_general_dilated(x, w[:, None, :].astype(x.dtype), window_strides=(1,),
                                 padding=[(k // 2, k // 2)],
                                 dimension_numbers=('NWC', 'WIO', 'NWC'),
                                 feature_group_count=c)
    return y + b.astype(x.dtype)


def to_heads(t, h):
    b, s, c = t.shape
    return t.reshape(b, s, h, c // h).transpose(0, 2, 1, 3)


def head_rmsnorm(t, g):
    b, h, s, d = t.shape
    t = t.transpose(0, 2, 1, 3)
    y = t * lax.rsqrt(jnp.mean(t * t, axis=-1, keepdims=True) + EPS)
    return y.reshape(b, s, h * d) * g.astype(jnp.float32)


def split_cols(t, sizes):
    out, o = [], 0
    for n in sizes:
        out.append(t[..., o:o + n])
        o += n
    return out


def to_chunks(t, l):
    nc = t.shape[2] // l
    return jnp.moveaxis(t.reshape(t.shape[:2] + (nc, l) + t.shape[3:]), 2, 0)


def from_chunks(t):
    t = jnp.moveaxis(t, 0, 2)
    return t.reshape(t.shape[:2] + (t.shape[2] * t.shape[3],) + t.shape[4:])


def mlstm_scan(q, k, v, log_i, log_f):
    b_, h_, s_, d = q.shape
    l = ML_CHUNK
    causal = jnp.tril(jnp.ones((l, l), dtype=bool))

    def step(carry, inp):
        c_st, n_st, m_st = carry
        qc, kc, vc, ic, fc = inp
        bcum = jnp.cumsum(fc, axis=-1)
        log_d = jnp.where(causal, bcum[..., :, None] - bcum[..., None, :] + ic[..., None, :], -jnp.inf)
        m_inter = bcum + m_st[..., None]
        m_t = jnp.maximum(m_inter, jnp.max(log_d, axis=-1))
        dmat = jnp.exp(log_d - m_t[..., None])
        sc = jnp.einsum('bhtd,bhsd->bhts', qc, kc) * dmat
        w_inter = jnp.exp(m_inter - m_t)
        num = jnp.einsum('bhts,bhsv->bhtv', sc, vc) + w_inter[..., None] * jnp.einsum('bhtk,bhkv->bhtv', qc, c_st)
        den = jnp.sum(sc, axis=-1) + w_inter * jnp.einsum('bhtk,bhk->bht', qc, n_st)
        hc = num / jnp.maximum(jnp.abs(den), jnp.exp(-m_t))[..., None]
        g_end = bcum[..., -1]
        log_w = g_end[..., None] - bcum + ic
        m_new = jnp.maximum(g_end + m_st, jnp.max(log_w, axis=-1))
        wgt = jnp.exp(log_w - m_new[..., None])
        decay = jnp.exp(g_end + m_st - m_new)
        c_new = decay[..., None, None] * c_st + jnp.einsum('bhs,bhsk,bhsv->bhkv', wgt, kc, vc)
        n_new = decay[..., None] * n_st + jnp.einsum('bhs,bhsk->bhk', wgt, kc)
        return (c_new, n_new, m_new), hc

    init = (jnp.zeros((b_, h_, d, d), jnp.float32), jnp.zeros((b_, h_, d), jnp.float32),
            jnp.zeros((b_, h_), jnp.float32))
    xs = (to_chunks(q, l), to_chunks(k, l), to_chunks(v, l), to_chunks(log_i, l), to_chunks(log_f, l))
    _, hs = lax.scan(step, init, xs)
    return from_chunks(hs)


def hgrn2_scan(q, k, v, log_f):
    b_, h_, s_, dk = q.shape
    dv = v.shape[-1]
    l = HG_CHUNK
    causal = jnp.tril(jnp.ones((l, l), dtype=bool))[:, :, None]

    def step(s_st, inp):
        qc, kc, vc, fc = inp
        bcum = jnp.cumsum(fc, axis=-2)
        rel = bcum[:, :, :, None, :] - bcum[:, :, None, :, :]
        dec = jnp.exp(jnp.where(causal, rel, -jnp.inf))
        att = jnp.einsum('bhtk,bhtsk,bhsk->bhts', qc, dec, kc)
        oc = jnp.einsum('bhts,bhsv->bhtv', att, vc) + jnp.einsum('bhtk,bhkv->bhtv', qc * jnp.exp(bcum), s_st)
        b_end = bcum[:, :, -1]
        kd = kc * jnp.exp(b_end[:, :, None] - bcum)
        s_new = jnp.exp(b_end)[..., None] * s_st + jnp.einsum('bhsk,bhsv->bhkv', kd, vc)
        return s_new, oc

    init = jnp.zeros((b_, h_, dk, dv), jnp.float32)
    xs = (to_chunks(q, l), to_chunks(k, l), to_chunks(v, l), to_chunks(log_f, l))
    _, os_ = lax.scan(step, init, xs)
    return from_chunks(os_)


def flip(t):
    return jnp.flip(t, axis=2)


def setup_inputs(seed: int = 0) -> dict:
    key = jax.random.key(seed)
    ks = jax.random.split(key, 26)
    f32 = jnp.float32

    def nrm(k, shape, scale):
        return jax.random.normal(k, shape, f32) * scale

    def gain(k, shape):
        return 1.0 + 0.02 * jax.random.normal(k, shape, f32)

    fbias_mask = jnp.array([0.0, 1.0, 0.0, 1.0], f32)[:, None]
    ml_gate_b = (fbias_mask * jnp.linspace(3.0, 6.0, ML_HEADS, dtype=f32)[None, :])[None] \
        + nrm(ks[6], (DEPTH, 4, ML_HEADS), 0.1)
    return {
        "x": nrm(ks[0], (BATCH, SEQ, D_MODEL), 1.0),
        "mem": nrm(ks[1], (BATCH, N_MEM, D_MODEL), 1.0),
        "norm_mix_g": gain(ks[2], (DEPTH, D_MODEL)),
        "w_in": nrm(ks[3], (DEPTH, D_MODEL, IN_DIM), D_MODEL ** -0.5),
        "ml_conv_w": nrm(ks[4], (DEPTH, CONV_W, 2 * ML_DIM), CONV_W ** -0.5),
        "ml_conv_b": nrm(ks[5], (DEPTH, 2 * ML_DIM), 0.02),
        "ml_gate_b": ml_gate_b,
        "ml_norm_g": gain(ks[7], (DEPTH, ML_DIM)),
        "hg_lb": nrm(ks[8], (2, DEPTH + 1, HG_DIM), 0.1),
        "hg_norm_g": gain(ks[9], (DEPTH, HG_DIM)),
        "w_branch_m": nrm(ks[10], (DEPTH, ML_DIM, D_MODEL), ML_DIM ** -0.5),
        "w_branch_h": nrm(ks[11], (DEPTH, HG_DIM, D_MODEL), HG_DIM ** -0.5),
        "w_out": nrm(ks[12], (DEPTH, D_MODEL, D_MODEL), D_MODEL ** -0.5),
        "norm_x_g": gain(ks[13], (DEPTH, D_MODEL)),
        "norm_mem_g": gain(ks[14], (DEPTH, D_MODEL)),
        "w_q_x": nrm(ks[15], (DEPTH, D_MODEL, D_MODEL), D_MODEL ** -0.5),
        "w_kv_x": nrm(ks[16], (DEPTH, D_MODEL, 2 * D_MODEL), D_MODEL ** -0.5),
        "w_o_x": nrm(ks[17], (DEPTH, D_MODEL, D_MODEL), D_MODEL ** -0.5),
        "norm_ffn_g": gain(ks[18], (DEPTH, D_MODEL)),
        "w_up": nrm(ks[19], (DEPTH, D_MODEL, 2 * D_FF), D_MODEL ** -0.5),
        "ffn_conv_w": nrm(ks[20], (DEPTH, CONV_W, D_FF), CONV_W ** -0.5),
        "ffn_conv_b": nrm(ks[21], (DEPTH, D_FF), 0.02),
        "w_down": nrm(ks[22], (DEPTH, D_FF, D_MODEL), D_FF ** -0.5),
        "norm_final_g": gain(ks[23], (D_MODEL,)),
    }


def reference(x, mem, norm_mix_g, w_in, ml_conv_w, ml_conv_b, ml_gate_b, ml_norm_g, hg_lb,
              hg_norm_g, w_branch_m, w_branch_h, w_out, norm_x_g, norm_mem_g, w_q_x, w_kv_x,
              w_o_x, norm_ffn_g, w_up, ffn_conv_w, ffn_conv_b, w_down, norm_final_g):
    f32 = jnp.float32
    dt = x.dtype
    bsz, seq, _ = x.shape
    lb_all = jnp.cumsum(jax.nn.softmax(hg_lb.astype(f32), axis=1), axis=1)

    for l in range(DEPTH):
        z = rmsnorm(x, norm_mix_g[l])
        proj = z @ w_in[l]
        (ml_qk, ml_v, ml_o, ml_g, hg_q, hg_ff, hg_fb, hg_i, hg_g,
         gate_m, gate_h) = split_cols(proj, IN_SPLITS)

        qk = jax.nn.silu(dwconv(ml_qk, ml_conv_w[l], ml_conv_b[l]))
        q_m = to_heads(qk[..., :ML_DIM], ML_HEADS).astype(f32)
        k_m = to_heads(qk[..., ML_DIM:], ML_HEADS).astype(f32) * (ML_HEAD_DIM ** -0.5)
        v_m = to_heads(ml_v, ML_HEADS).astype(f32)
        gates = (ml_g.reshape(bsz, seq, 4, ML_HEADS) + ml_gate_b[l].astype(dt)).astype(f32)
        gates = gates.transpose(2, 0, 3, 1)
        h_fwd = mlstm_scan(q_m, k_m, v_m, gates[0], jax.nn.log_sigmoid(gates[1]))
        h_bwd = flip(mlstm_scan(flip(q_m), flip(k_m), flip(v_m), flip(gates[2]),
                                flip(jax.nn.log_sigmoid(gates[3]))))
        y_m = (head_rmsnorm(h_fwd + h_bwd, ml_norm_g[l]) * jax.nn.sigmoid(ml_o.astype(f32))).astype(dt)

        lb = lb_all[:, l].reshape(2, HG_HEADS, 1, HG_EXPAND)
        q_h = to_heads(jax.nn.silu(hg_q), HG_HEADS).astype(f32)
        v_h = to_heads(hg_i, HG_HEADS).astype(f32)
        f_fwd = lb[0] + (1.0 - lb[0]) * jax.nn.sigmoid(to_heads(hg_ff, HG_HEADS).astype(f32))
        f_bwd = lb[1] + (1.0 - lb[1]) * jax.nn.sigmoid(to_heads(hg_fb, HG_HEADS).astype(f32))
        o_fwd = hgrn2_scan(q_h, 1.0 - f_fwd, v_h, jnp.log(f_fwd))
        o_bwd = flip(hgrn2_scan(flip(q_h), flip(1.0 - f_bwd), flip(v_h), flip(jnp.log(f_bwd))))
        y_h = (head_rmsnorm(o_fwd + o_bwd, hg_norm_g[l]) * jax.nn.silu(hg_g.astype(f32))).astype(dt)

        merged = jax.nn.sigmoid(gate_m) * (y_m @ w_branch_m[l]) + jax.nn.sigmoid(gate_h) * (y_h @ w_branch_h[l])
        x = x + merged @ w_out[l]

        u = rmsnorm(x, norm_x_g[l])
        mn = rmsnorm(mem, norm_mem_g[l])
        q_x = (u @ w_q_x[l]).reshape(bsz, seq, X_HEADS, X_HEAD_DIM)
        kv = (mn @ w_kv_x[l]).reshape(bsz, N_MEM, 2, X_HEADS, X_HEAD_DIM)
        scores = jnp.einsum('bshd,bmhd->bhsm', q_x.astype(f32), kv[:, :, 0].astype(f32)) * (X_HEAD_DIM ** -0.5)
        probs = jax.nn.softmax(scores, axis=-1).astype(dt)
        att = jnp.einsum('bhsm,bmhd->bshd', probs, kv[:, :, 1]).reshape(bsz, seq, D_MODEL)
        x = x + att @ w_o_x[l]

        u = rmsnorm(x, norm_ffn_g[l])
        up = u @ w_up[l]
        a = dwconv(up[..., :D_FF], ffn_conv_w[l], ffn_conv_b[l])
        x = x + (jax.nn.gelu(a) * up[..., D_FF:]) @ w_down[l]

    return rmsnorm(x, norm_final_g)
```

```python
import functools

import jax
import jax.numpy as jnp
from jax import lax
from jax.experimental import pallas as pl
from jax.experimental.pallas import tpu as pltpu

F32 = jnp.float32
BF16 = jnp.bfloat16

EPS = 1e-6
D_MODEL = 1024
ML_HEADS = 4
ML_HEAD_DIM = 256
ML_CHUNK = 128
HG_HEADS = 8
HG_DIM = 128
HG_CHUNK = 128
X_HEADS = 4
X_HEAD_DIM = 256
D_FF = 2816
FF_CHUNK = 1408

V7X_VMEM_LIMIT_BYTES = 56 * 1024 * 1024
SUBLANES = 8
LANES = 128

COL_ML = 0
COL_HG = COL_ML + 4 * 1024
COL_GATE = COL_HG + 5 * 1024
COL_MLG = COL_GATE + 2 * 1024
PROJ_COLS = COL_MLG + ML_HEADS * LANES


def _params(*sem):
    return pltpu.CompilerParams(dimension_semantics=sem, vmem_limit_bytes=V7X_VMEM_LIMIT_BYTES)


def _sigmoid(x):
    return 1.0 / (1.0 + jnp.exp(-x))


def _log_sigmoid(x):
    return jnp.minimum(x, 0.0) - jnp.log(1.0 + jnp.exp(-jnp.abs(x)))


def _rms(x, g):
    return x * lax.rsqrt(jnp.mean(x * x, axis=-1, keepdims=True) + EPS) * g


def _dot(a, b):
    return jnp.dot(a, b, preferred_element_type=F32)


def _dot_nt(a, b):
    return lax.dot_general(a, b, (((1,), (1,)), ((), ())), preferred_element_type=F32)


def _dot_tn(a, b):
    return lax.dot_general(a, b, (((0,), (0,)), ((), ())), preferred_element_type=F32)


def _norm_matmul_kernel(x_ref, g_ref, w_ref, o_ref, z_ref):
    @pl.when(pl.program_id(1) == 0)
    def _():
        z_ref[...] = _rms(x_ref[...], g_ref[...]).astype(BF16)

    o_ref[...] = _dot(z_ref[...], w_ref[...]).astype(o_ref.dtype)


def _norm_matmul(x, g, w, out_dtype, tm, tn):
    n, k = x.shape
    m = w.shape[1]
    tm = min(tm, n)
    tn = min(tn, m)
    return pl.pallas_call(
        _norm_matmul_kernel,
        grid=(n // tm, m // tn),
        in_specs=[
            pl.BlockSpec((tm, k), lambda i, j: (i, 0)),
            pl.BlockSpec((1, k), lambda i, j: (0, 0)),
            pl.BlockSpec((k, tn), lambda i, j: (0, j)),
        ],
        out_specs=pl.BlockSpec((tm, tn), lambda i, j: (i, j)),
        out_shape=jax.ShapeDtypeStruct((n, m), out_dtype),
        scratch_shapes=[pltpu.VMEM((tm, k), BF16)],
        compiler_params=_params("arbitrary", "arbitrary"),
        name="norm_matmul",
    )(x, g.reshape(1, k), w)


def _conv_silu(ref, cw_ref, cb_ref, r0, c, nc):
    l = ML_CHUNK
    s = ref.shape[0]
    cur = ref[pl.ds(r0, l), :]
    prev8 = ref[pl.ds(pl.multiple_of(jnp.maximum(r0 - SUBLANES, 0), SUBLANES), SUBLANES), :]
    next8 = ref[pl.ds(pl.multiple_of(jnp.minimum(r0 + l, s - SUBLANES), SUBLANES), SUBLANES), :]
    prev_row = jnp.where(c > 0, prev8[SUBLANES - 1:SUBLANES, :], 0.0)
    next_row = jnp.where(c < nc - 1, next8[0:1, :], 0.0)
    row = lax.broadcasted_iota(jnp.int32, cur.shape, 0)
    x_prev = jnp.where(row == 0, prev_row, pltpu.roll(cur, 1, axis=0))
    x_next = jnp.where(row == l - 1, next_row, pltpu.roll(cur, l - 1, axis=0))
    a = cw_ref[0:1, :] * x_prev + cw_ref[1:2, :] * cur + cw_ref[2:3, :] * x_next + cb_ref[...]
    return a * _sigmoid(a)


def _mlstm_chunk(c, rev, nc, q_ref, k_ref, v_ref, og_ref, g_ref, cwq_ref, cwk_ref, cbq_ref, cbk_ref,
                 gb_ref, ng_ref, out_ref, c_ref, n_ref, m_ref):
    l = ML_CHUNK
    r0 = pl.multiple_of(c * l, l)
    qc = _conv_silu(q_ref, cwq_ref, cbq_ref, r0, c, nc)
    kc = _conv_silu(k_ref, cwk_ref, cbk_ref, r0, c, nc) * (ML_HEAD_DIM ** -0.5)
    vc = v_ref[pl.ds(r0, l), :].astype(BF16)
    qb = qc.astype(BF16)

    g = g_ref[pl.ds(r0, l), :] + gb_ref[...]
    gt = g.T
    ic, fc = (2, 3) if rev else (0, 1)
    i_row = gt[ic:ic + 1, :]
    f_row = _log_sigmoid(gt[fc:fc + 1, :])
    i_col = g[:, ic:ic + 1]
    f_col = _log_sigmoid(g[:, fc:fc + 1])

    t_idx = lax.broadcasted_iota(jnp.int32, (l, l), 0)
    s_idx = lax.broadcasted_iota(jnp.int32, (l, l), 1)
    tri = (s_idx >= t_idx) if rev else (s_idx <= t_idx)
    tri_t = (t_idx >= s_idx) if rev else (t_idx <= s_idx)
    bcum_col = jnp.sum(jnp.where(tri, f_row, 0.0), axis=1, keepdims=True)
    bcum_row = jnp.sum(jnp.where(tri_t, f_col, 0.0), axis=0, keepdims=True)
    g_end = jnp.sum(f_row, axis=1, keepdims=True)

    m_prev = m_ref[0:1, 0:1]
    log_d = jnp.where(tri, bcum_col - bcum_row + i_row, -jnp.inf)
    m_inter = bcum_col + m_prev
    m_t = jnp.maximum(m_inter, jnp.max(log_d, axis=1, keepdims=True))
    dmat = jnp.exp(log_d - m_t)
    sc = _dot_nt(qb, kc.astype(BF16)) * dmat
    w_inter = jnp.exp(m_inter - m_t)
    c_st = c_ref[...]
    num = _dot(sc.astype(BF16), vc) + w_inter * _dot(qb, c_st.astype(BF16))
    n_st = n_ref[...]
    den = jnp.sum(sc, axis=1, keepdims=True) + w_inter * jnp.sum(qc * n_st, axis=1, keepdims=True)
    hc = num / jnp.maximum(jnp.abs(den), jnp.exp(-m_t))

    log_w = g_end - bcum_col + i_col
    m_new = jnp.maximum(g_end + m_prev, jnp.max(log_w, axis=0, keepdims=True))
    wk = jnp.exp(log_w - m_new) * kc
    decay = jnp.exp(g_end + m_prev - m_new)
    c_ref[...] = decay * c_st + _dot_tn(wk.astype(BF16), vc)
    n_ref[...] = decay * n_st + jnp.sum(wk, axis=0, keepdims=True)
    m_ref[...] = jnp.broadcast_to(m_new, m_ref.shape)

    rows = pl.ds(r0, l)
    if not rev:
        out_ref[rows, :] = hc
    else:
        hs = out_ref[rows, :] + hc
        y = hs * lax.rsqrt(jnp.mean(hs * hs, axis=-1, keepdims=True) + EPS) * ng_ref[...]
        out_ref[rows, :] = y * _sigmoid(og_ref[rows, :])


def _mlstm_kernel(q_ref, k_ref, v_ref, og_ref, g_ref, cwq_ref, cwk_ref, cbq_ref, cbk_ref,
                  gb_ref, ng_ref, out_ref, c_ref, n_ref, m_ref):
    nc = q_ref.shape[0] // ML_CHUNK
    refs = (q_ref, k_ref, v_ref, og_ref, g_ref, cwq_ref, cwk_ref, cbq_ref, cbk_ref,
            gb_ref, ng_ref, out_ref, c_ref, n_ref, m_ref)
    for rev in (False, True):
        c_ref[...] = jnp.zeros_like(c_ref)
        n_ref[...] = jnp.zeros_like(n_ref)
        m_ref[...] = jnp.zeros_like(m_ref)

        def body(i, carry, rev=rev):
            c = (nc - 1 - i) if rev else i
            _mlstm_chunk(c, rev, nc, *refs)
            return carry

        lax.fori_loop(0, nc, body, 0)


def _mlstm(proj, conv_w, conv_b, gate_b, norm_g, bsz, seq):
    n = proj.shape[0]
    hd = ML_HEAD_DIM
    blk = lambda off: pl.BlockSpec((seq, hd), lambda b, h, off=off: (b, off + h))
    wblk = lambda rows, off: pl.BlockSpec((rows, hd), lambda b, h, off=off: (0, off + h))
    return pl.pallas_call(
        _mlstm_kernel,
        grid=(bsz, ML_HEADS),
        in_specs=[
            blk(0), blk(ML_HEADS), blk(2 * ML_HEADS), blk(3 * ML_HEADS),
            pl.BlockSpec((seq, LANES), lambda b, h: (b, COL_MLG // LANES + h)),
            wblk(3, 0), wblk(3, ML_HEADS), wblk(1, 0), wblk(1, ML_HEADS),
            pl.BlockSpec((1, LANES), lambda b, h: (0, h)),
            wblk(1, 0),
        ],
        out_specs=pl.BlockSpec((seq, hd), lambda b, h: (b, h)),
        out_shape=jax.ShapeDtypeStruct((n, ML_HEADS * hd), F32),
        scratch_shapes=[pltpu.VMEM((hd, hd), F32), pltpu.VMEM((1, hd), F32), pltpu.VMEM((1, LANES), F32)],
        compiler_params=_params("arbitrary", "arbitrary"),
        name="mlstm",
    )(proj, proj, proj, proj, proj, conv_w, conv_w, conv_b, conv_b, gate_b, norm_g)


def _cumsum_rows(x, rev):
    l = x.shape[0]
    row = lax.broadcasted_iota(jnp.int32, x.shape, 0)
    sh = 1
    while sh < l:
        if rev:
            x = x + jnp.where(row < l - sh, pltpu.roll(x, l - sh, axis=0), 0.0)
        else:
            x = x + jnp.where(row >= sh, pltpu.roll(x, sh, axis=0), 0.0)
        sh *= 2
    return x


def _level_reference(b, w, rev, row):
    l = b.shape[0]
    if w == 1:
        if rev:
            return jnp.where((row & 1) == 1, b, pltpu.roll(b, l - 1, axis=0))
        return jnp.where((row & 1) == 1, pltpu.roll(b, 1, axis=0), b)
    if w == 2:
        p = row & 3
        if rev:
            return jnp.where(p == 0, pltpu.roll(b, l - 2, axis=0),
                             jnp.where(p == 1, pltpu.roll(b, l - 1, axis=0),
                                       jnp.where(p == 2, b, pltpu.roll(b, 1, axis=0))))
        return jnp.where(p == 0, pltpu.roll(b, l - 1, axis=0),
                         jnp.where(p == 1, b,
                                   jnp.where(p == 2, pltpu.roll(b, 1, axis=0), pltpu.roll(b, 2, axis=0))))
    pieces = []
    for start in range(0, l, 2 * w):
        r = start + w if rev else start + w - 1
        pieces.append(jnp.broadcast_to(b[r:r + 1, :], (2 * w, b.shape[1])))
    return pieces[0] if len(pieces) == 1 else jnp.concatenate(pieces, axis=0)


def _hgrn_chunk(c, rev, q_ref, f_ref, v_ref, gg_ref, lb_ref, ng_ref, out_ref, st_ref):
    l = HG_CHUNK
    r0 = pl.multiple_of(c * l, l)
    rows = pl.ds(r0, l)
    lb = lb_ref[1:2, :] if rev else lb_ref[0:1, :]
    f = lb + (1.0 - lb) * _sigmoid(f_ref[rows, :])
    kk = 1.0 - f
    qr = q_ref[rows, :]
    q = qr * _sigmoid(qr)
    vb = v_ref[rows, :].astype(BF16)
    b = _cumsum_rows(jnp.log(f), rev)

    row = lax.broadcasted_iota(jnp.int32, (l, HG_DIM), 0)
    t_idx = lax.broadcasted_iota(jnp.int32, (l, l), 0)
    s_idx = lax.broadcasted_iota(jnp.int32, (l, l), 1)
    diff = t_idx ^ s_idx
    att = None
    w = l // 2
    while w >= 1:
        r = _level_reference(b, w, rev, row)
        late = (row & w) != 0
        q_side = (~late) if rev else late
        e = jnp.exp(jnp.where(q_side, b - r, r - b))
        qt = jnp.where(q_side, q * e, 0.0).astype(BF16)
        kt = jnp.where(q_side, 0.0, kk * e).astype(BF16)
        a = _dot_nt(qt, kt)
        att = a if att is None else jnp.where(diff < 2 * w, a, att)
        w //= 2
    att = jnp.where(diff == 0, jnp.sum(q * kk, axis=1, keepdims=True), att)

    st = st_ref[...]
    o = _dot(att.astype(BF16), vb) + _dot_nt((q * jnp.exp(b)).astype(BF16), st.astype(BF16))
    b_end = b[0:1, :] if rev else b[l - 1:l, :]
    kd = (kk * jnp.exp(b_end - b)).astype(BF16)
    st_ref[...] = jnp.exp(b_end) * st + _dot_tn(vb, kd)

    if not rev:
        out_ref[rows, :] = o
    else:
        hs = out_ref[rows, :] + o
        y = hs * lax.rsqrt(jnp.mean(hs * hs, axis=-1, keepdims=True) + EPS) * ng_ref[...]
        gg = gg_ref[rows, :]
        out_ref[rows, :] = y * (gg * _sigmoid(gg))


def _hgrn_kernel(q_ref, ff_ref, fb_ref, v_ref, gg_ref, lb_ref, ng_ref, out_ref, st_ref):
    nc = q_ref.shape[0] // HG_CHUNK
    for rev in (False, True):
        st_ref[...] = jnp.zeros_like(st_ref)
        f_ref = fb_ref if rev else ff_ref

        def body(i, carry, rev=rev, f_ref=f_ref):
            c = (nc - 1 - i) if rev else i
            _hgrn_chunk(c, rev, q_ref, f_ref, v_ref, gg_ref, lb_ref, ng_ref, out_ref, st_ref)
            return carry

        lax.fori_loop(0, nc, body, 0)


def _hgrn(proj, lb, norm_g, bsz, seq):
    n = proj.shape[0]
    base = COL_HG // HG_DIM
    blk = lambda grp: pl.BlockSpec((seq, HG_DIM), lambda b, h, grp=grp: (b, base + grp * HG_HEADS + h))
    return pl.pallas_call(
        _hgrn_kernel,
        grid=(bsz, HG_HEADS),
        in_specs=[
            blk(0), blk(1), blk(2), blk(3), blk(4),
            pl.BlockSpec((2, HG_DIM), lambda b, h: (0, h)),
            pl.BlockSpec((1, HG_DIM), lambda b, h: (0, h)),
        ],
        out_specs=pl.BlockSpec((seq, HG_DIM), lambda b, h: (b, h)),
        out_shape=jax.ShapeDtypeStruct((n, HG_HEADS * HG_DIM), F32),
        scratch_shapes=[pltpu.VMEM((HG_DIM, HG_DIM), F32)],
        compiler_params=_params("arbitrary", "arbitrary"),
        name="hgrn2",
    )(proj, proj, proj, proj, proj, lb, norm_g)


def _merge_kernel(x_ref, ym_ref, yh_ref, gm_ref, gh_ref, wm_ref, wh_ref, wo_ref, o_ref):
    pm = _dot(ym_ref[...].astype(BF16), wm_ref[...])
    ph = _dot(yh_ref[...].astype(BF16), wh_ref[...])
    merged = _sigmoid(gm_ref[...]) * pm + _sigmoid(gh_ref[...]) * ph
    o_ref[...] = x_ref[...] + _dot(merged.astype(BF16), wo_ref[...])


def _merge(x, y_m, y_h, proj, w_m, w_h, w_o, tm):
    n, d = x.shape
    tm = min(tm, n)
    row = lambda cb: pl.BlockSpec((tm, d), lambda i, cb=cb: (i, cb))
    full = pl.BlockSpec((d, d), lambda i: (0, 0))
    return pl.pallas_call(
        _merge_kernel,
        grid=(n // tm,),
        in_specs=[row(0), row(0), row(0), row(COL_GATE // d), row(COL_GATE // d + 1), full, full, full],
        out_specs=row(0),
        out_shape=jax.ShapeDtypeStruct((n, d), F32),
        compiler_params=_params("arbitrary"),
        name="merge",
    )(x, y_m, y_h, proj, proj, w_m, w_h, w_o)


def _xattn_kernel(x_ref, g_ref, wq_ref, kv_ref, wo_ref, o_ref):
    x = x_ref[...]
    u = _rms(x, g_ref[...]).astype(BF16)
    q = _dot(u, wq_ref[...])
    heads = []
    for h in range(X_HEADS):
        lo = h * X_HEAD_DIM
        qh = q[:, lo:lo + X_HEAD_DIM].astype(BF16)
        kh = kv_ref[:, lo:lo + X_HEAD_DIM]
        vh = kv_ref[:, D_MODEL + lo:D_MODEL + lo + X_HEAD_DIM]
        s = _dot_nt(qh, kh) * (X_HEAD_DIM ** -0.5)
        p = jnp.exp(s - jnp.max(s, axis=-1, keepdims=True))
        p = p / jnp.sum(p, axis=-1, keepdims=True)
        heads.append(_dot(p.astype(BF16), vh).astype(BF16))
    att = jnp.concatenate(heads, axis=-1)
    o_ref[...] = x + _dot(att, wo_ref[...])


def _xattn(x, g, w_q, kv, w_o, seq, n_mem, tm):
    n, d = x.shape
    tm = min(tm, seq)
    per_seq = seq // tm
    full = pl.BlockSpec((d, d), lambda i: (0, 0))
    return pl.pallas_call(
        _xattn_kernel,
        grid=(n // tm,),
        in_specs=[
            pl.BlockSpec((tm, d), lambda i: (i, 0)),
            pl.BlockSpec((1, d), lambda i: (0, 0)),
            full,
            pl.BlockSpec((n_mem, 2 * d), lambda i: (i // per_seq, 0)),
            full,
        ],
        out_specs=pl.BlockSpec((tm, d), lambda i: (i, 0)),
        out_shape=jax.ShapeDtypeStruct((n, d), F32),
        compiler_params=_params("arbitrary"),
        name="xattn",
    )(x, g.reshape(1, d), w_q, kv, w_o)


def _gelu(x):
    return 0.5 * x * (1.0 + jnp.tanh(0.7978845608028654 * (x + 0.044715 * (x * x * x))))


def _ffn_kernel(x_ref, xp_ref, xn_ref, g_ref, wup_ref, cw_ref, cb_ref, wdn_ref, gf_ref, o_ref, *,
                per_seq, final_norm):
    i = pl.program_id(0)
    tm = x_ref.shape[0]
    x = x_ref[...]
    g = g_ref[...]
    first = (i % per_seq) == 0
    last = (i % per_seq) == per_seq - 1
    up_prev = jnp.where(first, 0.0, _rms(xp_ref[...], g))
    up_next = jnp.where(last, 0.0, _rms(xn_ref[...], g))
    u_ext = jnp.concatenate([up_prev, _rms(x, g), up_next], axis=0).astype(BF16)
    ext = tm + 2 * SUBLANES
    acc = x
    for c0 in range(0, D_FF, FF_CHUNK):
        a_ext = _dot(u_ext, wup_ref[:, c0:c0 + FF_CHUNK])
        gate = _dot(u_ext, wup_ref[:, D_FF + c0:D_FF + c0 + FF_CHUNK])[SUBLANES:SUBLANES + tm, :]
        a_prev = pltpu.roll(a_ext, 1, axis=0)[SUBLANES:SUBLANES + tm, :]
        a_next = pltpu.roll(a_ext, ext - 1, axis=0)[SUBLANES:SUBLANES + tm, :]
        a_cur = a_ext[SUBLANES:SUBLANES + tm, :]
        a = (cw_ref[0:1, c0:c0 + FF_CHUNK] * a_prev + cw_ref[1:2, c0:c0 + FF_CHUNK] * a_cur
             + cw_ref[2:3, c0:c0 + FF_CHUNK] * a_next + cb_ref[:, c0:c0 + FF_CHUNK])
        acc = acc + _dot((_gelu(a) * gate).astype(BF16), wdn_ref[c0:c0 + FF_CHUNK, :])
    o_ref[...] = _rms(acc, gf_ref[...]) if final_norm else acc


def _ffn(x, g, w_up, conv_w, conv_b, w_dn, g_final, final_norm, seq, tm):
    n, d = x.shape
    tm = min(tm, seq)
    per_seq = seq // tm
    tb = tm // SUBLANES
    nb = n // SUBLANES
    const = lambda shape: pl.BlockSpec(shape, lambda i: (0, 0))
    return pl.pallas_call(
        functools.partial(_ffn_kernel, per_seq=per_seq, final_norm=final_norm),
        grid=(n // tm,),
        in_specs=[
            pl.BlockSpec((tm, d), lambda i: (i, 0)),
            pl.BlockSpec((SUBLANES, d), lambda i: (jnp.maximum(i * tb - 1, 0), 0)),
            pl.BlockSpec((SUBLANES, d), lambda i: (jnp.minimum((i + 1) * tb, nb - 1), 0)),
            const((1, d)),
            const((d, 2 * D_FF)),
            const((3, D_FF)),
            const((1, D_FF)),
            const((D_FF, d)),
            const((1, d)),
        ],
        out_specs=pl.BlockSpec((tm, d), lambda i: (i, 0)),
        out_shape=jax.ShapeDtypeStruct((n, d), F32),
        compiler_params=_params("arbitrary"),
        name="ffn",
    )(x, x, x, g.reshape(1, d), w_up, conv_w, conv_b.reshape(1, D_FF), w_dn, g_final.reshape(1, d))


def kernel(x, mem, norm_mix_g, w_in, ml_conv_w, ml_conv_b, ml_gate_b, ml_norm_g, hg_lb, hg_norm_g,
           w_branch_m, w_branch_h, w_out, norm_x_g, norm_mem_g, w_q_x, w_kv_x, w_o_x, norm_ffn_g,
           w_up, ffn_conv_w, ffn_conv_b, w_down, norm_final_g):
    bsz, seq, d = x.shape
    n_mem = mem.shape[1]
    depth = w_in.shape[0]
    n = bsz * seq
    xf = x.reshape(n, d)
    memf = mem.reshape(bsz * n_mem, d)
    lb_all = jnp.cumsum(jax.nn.softmax(hg_lb.astype(F32), axis=1), axis=1)

    for l in range(depth):
        w = w_in[l]
        wg = w[:, 4096:4096 + 4 * ML_HEADS].reshape(d, 4, ML_HEADS).transpose(0, 2, 1)
        wg = jnp.pad(wg, ((0, 0), (0, 0), (0, LANES - 4))).reshape(d, ML_HEADS * LANES)
        w_all = jnp.concatenate([w[:, :4096], w[:, 4096 + 4 * ML_HEADS:], wg], axis=1).astype(BF16)
        gate_b = jnp.pad(ml_gate_b[l].T, ((0, 0), (0, LANES - 4))).reshape(1, ML_HEADS * LANES)

        proj = _norm_matmul(xf, norm_mix_g[l], w_all, F32, tm=2048, tn=512)
        y_m = _mlstm(proj, ml_conv_w[l], ml_conv_b[l].reshape(1, -1), gate_b,
                     ml_norm_g[l].reshape(1, -1), bsz, seq)
        y_h = _hgrn(proj, lb_all[:, l], hg_norm_g[l].reshape(1, -1), bsz, seq)
        xf = _merge(xf, y_m, y_h, proj, w_branch_m[l].astype(BF16), w_branch_h[l].astype(BF16),
                    w_out[l].astype(BF16), tm=512)

        kv = _norm_matmul(memf, norm_mem_g[l], w_kv_x[l].astype(BF16), BF16, tm=2048, tn=512)
        xf = _xattn(xf, norm_x_g[l], w_q_x[l].astype(BF16), kv, w_o_x[l].astype(BF16), seq, n_mem, tm=512)

        xf = _ffn(xf, norm_ffn_g[l], w_up[l].astype(BF16), ffn_conv_w[l], ffn_conv_b[l],
                  w_down[l].astype(BF16), norm_final_g, l == depth - 1, seq, tm=512)
    return xf.reshape(bsz, seq, d)
```

```python
import functools

import jax
import jax.numpy as jnp
from jax import lax
from jax.experimental import pallas as pl
from jax.experimental.pallas import tpu as pltpu

F32 = jnp.float32
BF16 = jnp.bfloat16

EPS = 1e-6
D_MODEL = 1024
ML_HEADS = 4
ML_HEAD_DIM = 256
ML_CHUNK = 128
HG_HEADS = 8
HG_DIM = 128
HG_CHUNK = 128
X_HEADS = 4
X_HEAD_DIM = 256
D_FF = 2816
FF_CHUNK = 1408
SCAN_UNROLL = 2

V7X_VMEM_LIMIT_BYTES = 56 * 1024 * 1024
SUBLANES = 8
LANES = 128

COL_ML = 0
COL_HG = COL_ML + 4 * 1024
COL_GATE = COL_HG + 5 * 1024
COL_MLG = COL_GATE + 2 * 1024
PROJ_COLS = COL_MLG + ML_HEADS * LANES


def _params(*sem):
    return pltpu.CompilerParams(dimension_semantics=sem, vmem_limit_bytes=V7X_VMEM_LIMIT_BYTES)


def _sigmoid(x):
    return 0.5 * jnp.tanh(0.5 * x) + 0.5


def _log_sigmoid(x):
    return jnp.minimum(x, 0.0) - jnp.log(1.0 + jnp.exp(-jnp.abs(x)))


def _rms(x, g):
    return x * lax.rsqrt(jnp.mean(x * x, axis=-1, keepdims=True) + EPS) * g


def _dot(a, b):
    return jnp.dot(a, b, preferred_element_type=F32)


def _dot_nt(a, b):
    return lax.dot_general(a, b, (((1,), (1,)), ((), ())), preferred_element_type=F32)


def _dot_tn(a, b):
    return lax.dot_general(a, b, (((0,), (0,)), ((), ())), preferred_element_type=F32)


def _run_scans(nc, load, compute, store, state_refs):
    half = nc // 2

    def loop(lo, hi, finalize):
        def body(i, carry):
            steps = [(i * SCAN_UNROLL + j, rev) for j in range(SCAN_UNROLL) for rev in (False, True)]
            chunks = [(nc - 1 - s) if rev else s for s, rev in steps]
            loaded = [load(c, rev, finalize) for c, (_, rev) in zip(chunks, steps)]
            states = [[r[...] for r in refs] for refs in state_refs]
            gens = [compute(vals, rev, finalize, states) for vals, (_, rev) in zip(loaded, steps)]
            outs = [None] * len(gens)
            live = list(range(len(gens)))
            while live:
                for k in list(live):
                    try:
                        next(gens[k])
                    except StopIteration as done:
                        outs[k] = done.value
                        live.remove(k)
            for c, out in zip(chunks, outs):
                store(c, out)
            for refs, vals in zip(state_refs, states):
                for r, v in zip(refs, vals):
                    r[...] = v
            return carry

        lax.fori_loop(lo // SCAN_UNROLL, hi // SCAN_UNROLL, body, 0)

    loop(0, half, False)
    loop(half, nc, True)


def _norm_matmul_kernel(x_ref, g_ref, w_ref, o_ref, z_ref):
    @pl.when(pl.program_id(1) == 0)
    def _():
        z_ref[...] = _rms(x_ref[...], g_ref[...]).astype(BF16)

    o_ref[...] = _dot(z_ref[...], w_ref[...]).astype(o_ref.dtype)


def _norm_matmul(x, g, w, out_dtype, tm, tn):
    n, k = x.shape
    m = w.shape[1]
    tm = min(tm, n)
    tn = min(tn, m)
    return pl.pallas_call(
        _norm_matmul_kernel,
        grid=(n // tm, m // tn),
        in_specs=[
            pl.BlockSpec((tm, k), lambda i, j: (i, 0)),
            pl.BlockSpec((1, k), lambda i, j: (0, 0)),
            pl.BlockSpec((k, tn), lambda i, j: (0, j)),
        ],
        out_specs=pl.BlockSpec((tm, tn), lambda i, j: (i, j)),
        out_shape=jax.ShapeDtypeStruct((n, m), out_dtype),
        scratch_shapes=[pltpu.VMEM((tm, k), BF16)],
        compiler_params=_params("arbitrary", "arbitrary"),
        name="norm_matmul",
    )(x, g.reshape(1, k), w)


def _conv_silu(ref, cw_ref, cb_ref, r0, c, nc):
    l = ML_CHUNK
    s = ref.shape[0]
    cur = ref[pl.ds(r0, l), :]
    prev8 = ref[pl.ds(pl.multiple_of(jnp.maximum(r0 - SUBLANES, 0), SUBLANES), SUBLANES), :]
    next8 = ref[pl.ds(pl.multiple_of(jnp.minimum(r0 + l, s - SUBLANES), SUBLANES), SUBLANES), :]
    prev_row = jnp.where(c > 0, prev8[SUBLANES - 1:SUBLANES, :], 0.0)
    next_row = jnp.where(c < nc - 1, next8[0:1, :], 0.0)
    row = lax.broadcasted_iota(jnp.int32, cur.shape, 0)
    x_prev = jnp.where(row == 0, prev_row, pltpu.roll(cur, 1, axis=0))
    x_next = jnp.where(row == l - 1, next_row, pltpu.roll(cur, l - 1, axis=0))
    a = cw_ref[0:1, :] * x_prev + cw_ref[1:2, :] * cur + cw_ref[2:3, :] * x_next + cb_ref[...]
    return a * _sigmoid(a)


def _mlstm_chunk(vals, rev, finalize, states, gate_b, norm_g):
    l = ML_CHUNK
    d = int(rev)
    qb, kb, v, g = vals[:4]
    m_st = states[d][2]
    qc = qb.astype(F32)
    kc = kb.astype(F32)
    vc = v.astype(BF16)

    g = g + gate_b
    gt = g.T
    ic, fc = (2, 3) if rev else (0, 1)
    i_row = gt[ic:ic + 1, :]
    f_row = _log_sigmoid(gt[fc:fc + 1, :])
    i_col = g[:, ic:ic + 1]
    f_col = _log_sigmoid(g[:, fc:fc + 1])

    t_idx = lax.broadcasted_iota(jnp.int32, (l, l), 0)
    s_idx = lax.broadcasted_iota(jnp.int32, (l, l), 1)
    tri = (s_idx >= t_idx) if rev else (s_idx <= t_idx)
    tri_t = (t_idx >= s_idx) if rev else (t_idx <= s_idx)
    bcum_col = jnp.sum(jnp.where(tri, f_row, 0.0), axis=1, keepdims=True)
    bcum_row = jnp.sum(jnp.where(tri_t, f_col, 0.0), axis=0, keepdims=True)
    g_end = jnp.sum(f_row, axis=1, keepdims=True)

    m_prev = m_st[0:1, 0:1]
    log_d = jnp.where(tri, bcum_col - bcum_row + i_row, -jnp.inf)
    m_inter = bcum_col + m_prev
    m_t = jnp.maximum(m_inter, jnp.max(log_d, axis=1, keepdims=True))
    dmat = jnp.exp(log_d - m_t)
    w_inter = jnp.exp(m_inter - m_t)
    log_w = g_end - bcum_col + i_col
    m_new = jnp.maximum(g_end + m_prev, jnp.max(log_w, axis=0, keepdims=True))
    wk = jnp.exp(log_w - m_new) * kc
    decay = jnp.exp(g_end + m_prev - m_new)
    states[d][2] = jnp.broadcast_to(m_new, m_st.shape)
    qk = _dot_nt(qb, kb)
    yield

    c_st, n_st = states[d][0], states[d][1]
    q_c = _dot(qb, c_st.astype(BF16))
    q_n = jnp.sum(qc * n_st, axis=1, keepdims=True)
    states[d][0] = decay * c_st + _dot_tn(wk.astype(BF16), vc)
    states[d][1] = decay * n_st + jnp.sum(wk, axis=0, keepdims=True)
    yield

    sc = qk * dmat
    num = _dot(sc.astype(BF16), vc) + w_inter * q_c
    den = jnp.sum(sc, axis=1, keepdims=True) + w_inter * q_n
    hc = num * (1.0 / jnp.maximum(jnp.abs(den), jnp.exp(-m_t)))
    if finalize:
        yield
        prev, og = vals[4:]
        hs = prev + hc
        y = hs * lax.rsqrt(jnp.mean(hs * hs, axis=-1, keepdims=True) + EPS) * norm_g
        hc = y * _sigmoid(og)
    return hc


def _mlstm_kernel(q_ref, k_ref, v_ref, og_ref, g_ref, cwq_ref, cwk_ref, cbq_ref, cbk_ref,
                  gb_ref, ng_ref, out_ref, qs_ref, ks_ref, c_ref, n_ref, m_ref):
    l = ML_CHUNK
    nc = q_ref.shape[0] // l

    def conv_body(c, carry):
        r0 = pl.multiple_of(c * l, l)
        qs_ref[pl.ds(r0, l), :] = _conv_silu(q_ref, cwq_ref, cbq_ref, r0, c, nc).astype(BF16)
        kc = _conv_silu(k_ref, cwk_ref, cbk_ref, r0, c, nc) * (ML_HEAD_DIM ** -0.5)
        ks_ref[pl.ds(r0, l), :] = kc.astype(BF16)
        return carry

    lax.fori_loop(0, nc, conv_body, 0)
    c_ref[...] = jnp.zeros_like(c_ref)
    n_ref[...] = jnp.zeros_like(n_ref)
    m_ref[...] = jnp.zeros_like(m_ref)

    def rows_of(c):
        return pl.ds(pl.multiple_of(c * l, l), l)

    def load(c, rev, finalize):
        rows = rows_of(c)
        vals = (qs_ref[rows, :], ks_ref[rows, :], v_ref[rows, :], g_ref[rows, :])
        if finalize:
            vals += (out_ref[rows, :], og_ref[rows, :])
        return vals

    def compute(vals, rev, finalize, states):
        return _mlstm_chunk(vals, rev, finalize, states, gb_ref[...], ng_ref[...])

    def store(c, out):
        out_ref[rows_of(c), :] = out

    _run_scans(nc, load, compute, store, [(c_ref.at[d], n_ref.at[d], m_ref.at[d]) for d in (0, 1)])


def _mlstm(proj, conv_w, conv_b, gate_b, norm_g, bsz, seq):
    n = proj.shape[0]
    hd = ML_HEAD_DIM
    blk = lambda off: pl.BlockSpec((seq, hd), lambda b, h, off=off: (b, off + h))
    wblk = lambda rows, off: pl.BlockSpec((rows, hd), lambda b, h, off=off: (0, off + h))
    return pl.pallas_call(
        _mlstm_kernel,
        grid=(bsz, ML_HEADS),
        in_specs=[
            blk(0), blk(ML_HEADS), blk(2 * ML_HEADS), blk(3 * ML_HEADS),
            pl.BlockSpec((seq, LANES), lambda b, h: (b, COL_MLG // LANES + h)),
            wblk(3, 0), wblk(3, ML_HEADS), wblk(1, 0), wblk(1, ML_HEADS),
            pl.BlockSpec((1, LANES), lambda b, h: (0, h)),
            wblk(1, 0),
        ],
        out_specs=pl.BlockSpec((seq, hd), lambda b, h: (b, h)),
        out_shape=jax.ShapeDtypeStruct((n, ML_HEADS * hd), F32),
        scratch_shapes=[pltpu.VMEM((seq, hd), BF16), pltpu.VMEM((seq, hd), BF16),
                        pltpu.VMEM((2, hd, hd), F32), pltpu.VMEM((2, 1, hd), F32), pltpu.VMEM((2, 1, LANES), F32)],
        compiler_params=_params("arbitrary", "arbitrary"),
        name="mlstm",
    )(proj, proj, proj, proj, proj, conv_w, conv_w, conv_b, conv_b, gate_b, norm_g)


def _split3(x):
    hi = x.astype(BF16)
    r1 = x - hi.astype(F32)
    mid = r1.astype(BF16)
    lo = (r1 - mid.astype(F32)).astype(BF16)
    return jnp.concatenate([hi, mid, lo], axis=0)


def _tri3(l, rev):
    t = lax.broadcasted_iota(jnp.int32, (l, 3 * l), 0)
    u = lax.broadcasted_iota(jnp.int32, (l, 3 * l), 1) & (l - 1)
    inside = (u >= t) if rev else (u <= t)
    return jnp.where(inside, 1.0, 0.0).astype(BF16)


def _cumsum_rows(x, tri3):
    return _dot(tri3, _split3(x))


def _level_reference(b, w, rev, row):
    l = b.shape[0]
    if w == 2:
        p = row & 3
        if rev:
            return jnp.where(p == 0, pltpu.roll(b, l - 2, axis=0),
                             jnp.where(p == 1, pltpu.roll(b, l - 1, axis=0),
                                       jnp.where(p == 2, b, pltpu.roll(b, 1, axis=0))))
        return jnp.where(p == 0, pltpu.roll(b, l - 1, axis=0),
                         jnp.where(p == 1, b,
                                   jnp.where(p == 2, pltpu.roll(b, 1, axis=0), pltpu.roll(b, 2, axis=0))))
    pieces = []
    for start in range(0, l, 2 * w):
        r = start + w if rev else start + w - 1
        pieces.append(jnp.broadcast_to(b[r:r + 1, :], (2 * w, b.shape[1])))
    return pieces[0] if len(pieces) == 1 else jnp.concatenate(pieces, axis=0)


def _hgrn_chunk(vals, rev, finalize, states, lb, norm_g, tri3):
    l = HG_CHUNK
    d = int(rev)
    qr, fr, v = vals[:3]
    f = lb + (1.0 - lb) * _sigmoid(fr)
    kk = 1.0 - f
    q = qr * _sigmoid(qr)
    vb = v.astype(BF16)
    b = _cumsum_rows(jnp.log(f), tri3)
    yield

    st = states[d][0]
    o_inter = _dot_nt((q * jnp.exp(b)).astype(BF16), st.astype(BF16))
    b_end = b[0:1, :] if rev else b[l - 1:l, :]
    kd = (kk * jnp.exp(b_end - b)).astype(BF16)
    states[d][0] = jnp.exp(b_end) * st + _dot_tn(vb, kd)


    row = lax.broadcasted_iota(jnp.int32, (l, HG_DIM), 0)
    t_idx = lax.broadcasted_iota(jnp.int32, (l, l), 0)
    s_idx = lax.broadcasted_iota(jnp.int32, (l, l), 1)
    diff = t_idx ^ s_idx
    att = None
    w = l // 2
    while w >= 1:
        late = (row & w) != 0
        q_side = (~late) if rev else late
        if w == 1:
            m = jnp.where(q_side, q * f, kk)
        else:
            r = _level_reference(b, w, rev, row)
            m = jnp.where(q_side, q, kk) * jnp.exp(-jnp.abs(b - r))
        mb = m.astype(BF16)
        a = _dot_nt(mb, mb)
        att = a if att is None else jnp.where(diff < 2 * w, a, att)
        w //= 2
    yield

    keep = (s_idx > t_idx) if rev else (s_idx < t_idx)
    att = jnp.where(diff == 0, jnp.sum(q * kk, axis=1, keepdims=True), jnp.where(keep, att, 0.0))
    o = _dot(att.astype(BF16), vb) + o_inter
    if finalize:
        yield
        prev, gg = vals[3:]
        hs = prev + o
        y = hs * lax.rsqrt(jnp.mean(hs * hs, axis=-1, keepdims=True) + EPS) * norm_g
        o = y * (gg * _sigmoid(gg))
    return o


def _hgrn_kernel(q_ref, ff_ref, fb_ref, v_ref, gg_ref, lb_ref, ng_ref, out_ref, st_ref, tri_ref):
    l = HG_CHUNK
    nc = q_ref.shape[0] // l
    st_ref[...] = jnp.zeros_like(st_ref)
    tri_ref[0] = _tri3(l, False)
    tri_ref[1] = _tri3(l, True)

    def rows_of(c):
        return pl.ds(pl.multiple_of(c * l, l), l)

    def load(c, rev, finalize):
        rows = rows_of(c)
        vals = (q_ref[rows, :], (fb_ref if rev else ff_ref)[rows, :], v_ref[rows, :])
        if finalize:
            vals += (out_ref[rows, :], gg_ref[rows, :])
        return vals

    def compute(vals, rev, finalize, states):
        d = int(rev)
        return _hgrn_chunk(vals, rev, finalize, states, lb_ref[d:d + 1, :], ng_ref[...], tri_ref[d])

    def store(c, out):
        out_ref[rows_of(c), :] = out

    _run_scans(nc, load, compute, store, [(st_ref.at[0],), (st_ref.at[1],)])


def _hgrn(proj, lb, norm_g, bsz, seq):
    n = proj.shape[0]
    base = COL_HG // HG_DIM
    blk = lambda grp: pl.BlockSpec((seq, HG_DIM), lambda b, h, grp=grp: (b, base + grp * HG_HEADS + h))
    return pl.pallas_call(
        _hgrn_kernel,
        grid=(bsz, HG_HEADS),
        in_specs=[
            blk(0), blk(1), blk(2), blk(3), blk(4),
            pl.BlockSpec((2, HG_DIM), lambda b, h: (0, h)),
            pl.BlockSpec((1, HG_DIM), lambda b, h: (0, h)),
        ],
        out_specs=pl.BlockSpec((seq, HG_DIM), lambda b, h: (b, h)),
        out_shape=jax.ShapeDtypeStruct((n, HG_HEADS * HG_DIM), F32),
        scratch_shapes=[pltpu.VMEM((2, HG_DIM, HG_DIM), F32), pltpu.VMEM((2, HG_CHUNK, 3 * HG_CHUNK), BF16)],
        compiler_params=_params("arbitrary", "arbitrary"),
        name="hgrn2",
    )(proj, proj, proj, proj, proj, lb, norm_g)


def _merge_kernel(x_ref, ym_ref, yh_ref, gm_ref, gh_ref, wm_ref, wh_ref, wo_ref, o_ref):
    pm = _dot(ym_ref[...].astype(BF16), wm_ref[...])
    ph = _dot(yh_ref[...].astype(BF16), wh_ref[...])
    merged = _sigmoid(gm_ref[...]) * pm + _sigmoid(gh_ref[...]) * ph
    o_ref[...] = x_ref[...] + _dot(merged.astype(BF16), wo_ref[...])


def _merge(x, y_m, y_h, proj, w_m, w_h, w_o, tm):
    n, d = x.shape
    tm = min(tm, n)
    row = lambda cb: pl.BlockSpec((tm, d), lambda i, cb=cb: (i, cb))
    full = pl.BlockSpec((d, d), lambda i: (0, 0))
    return pl.pallas_call(
        _merge_kernel,
        grid=(n // tm,),
        in_specs=[row(0), row(0), row(0), row(COL_GATE // d), row(COL_GATE // d + 1), full, full, full],
        out_specs=row(0),
        out_shape=jax.ShapeDtypeStruct((n, d), F32),
        compiler_params=_params("arbitrary"),
        name="merge",
    )(x, y_m, y_h, proj, proj, w_m, w_h, w_o)


def _xattn_kernel(x_ref, g_ref, wq_ref, kv_ref, wo_ref, o_ref):
    x = x_ref[...]
    u = _rms(x, g_ref[...]).astype(BF16)
    q = _dot(u, wq_ref[...])
    heads = []
    for h in range(X_HEADS):
        lo = h * X_HEAD_DIM
        qh = q[:, lo:lo + X_HEAD_DIM].astype(BF16)
        kh = kv_ref[:, lo:lo + X_HEAD_DIM]
        vh = kv_ref[:, D_MODEL + lo:D_MODEL + lo + X_HEAD_DIM]
        s = _dot_nt(qh, kh) * (X_HEAD_DIM ** -0.5)
        p = jnp.exp(s - jnp.max(s, axis=-1, keepdims=True))
        p = p * (1.0 / jnp.sum(p, axis=-1, keepdims=True))
        heads.append(_dot(p.astype(BF16), vh).astype(BF16))
    att = jnp.concatenate(heads, axis=-1)
    o_ref[...] = x + _dot(att, wo_ref[...])


def _xattn(x, g, w_q, kv, w_o, seq, n_mem, tm):
    n, d = x.shape
    tm = min(tm, seq)
    per_seq = seq // tm
    full = pl.BlockSpec((d, d), lambda i: (0, 0))
    return pl.pallas_call(
        _xattn_kernel,
        grid=(n // tm,),
        in_specs=[
            pl.BlockSpec((tm, d), lambda i: (i, 0)),
            pl.BlockSpec((1, d), lambda i: (0, 0)),
            full,
            pl.BlockSpec((n_mem, 2 * d), lambda i: (i // per_seq, 0)),
            full,
        ],
        out_specs=pl.BlockSpec((tm, d), lambda i: (i, 0)),
        out_shape=jax.ShapeDtypeStruct((n, d), F32),
        compiler_params=_params("arbitrary"),
        name="xattn",
    )(x, g.reshape(1, d), w_q, kv, w_o)


def _gelu(x):
    return 0.5 * x * (1.0 + jnp.tanh(0.7978845608028654 * (x + 0.044715 * (x * x * x))))


def _ffn_kernel(x_ref, xp_ref, xn_ref, g_ref, wup_ref, cw_ref, cb_ref, wdn_ref, gf_ref, o_ref, *,
                per_seq, final_norm):
    i = pl.program_id(0)
    tm = x_ref.shape[0]
    x = x_ref[...]
    g = g_ref[...]
    first = (i % per_seq) == 0
    last = (i % per_seq) == per_seq - 1
    up_prev = jnp.where(first, 0.0, _rms(xp_ref[...], g))
    up_next = jnp.where(last, 0.0, _rms(xn_ref[...], g))
    u_ext = jnp.concatenate([up_prev, _rms(x, g), up_next], axis=0).astype(BF16)
    ext = tm + 2 * SUBLANES
    acc = x
    for c0 in range(0, D_FF, FF_CHUNK):
        a_ext = _dot(u_ext, wup_ref[:, c0:c0 + FF_CHUNK])
        gate = _dot(u_ext, wup_ref[:, D_FF + c0:D_FF + c0 + FF_CHUNK])[SUBLANES:SUBLANES + tm, :]
        a_prev = pltpu.roll(a_ext, 1, axis=0)[SUBLANES:SUBLANES + tm, :]
        a_next = pltpu.roll(a_ext, ext - 1, axis=0)[SUBLANES:SUBLANES + tm, :]
        a_cur = a_ext[SUBLANES:SUBLANES + tm, :]
        a = (cw_ref[0:1, c0:c0 + FF_CHUNK] * a_prev + cw_ref[1:2, c0:c0 + FF_CHUNK] * a_cur
             + cw_ref[2:3, c0:c0 + FF_CHUNK] * a_next + cb_ref[:, c0:c0 + FF_CHUNK])
        acc = acc + _dot((_gelu(a) * gate).astype(BF16), wdn_ref[c0:c0 + FF_CHUNK, :])
    o_ref[...] = _rms(acc, gf_ref[...]) if final_norm else acc


def _ffn(x, g, w_up, conv_w, conv_b, w_dn, g_final, final_norm, seq, tm):
    n, d = x.shape
    tm = min(tm, seq)
    per_seq = seq // tm
    tb = tm // SUBLANES
    nb = n // SUBLANES
    const = lambda shape: pl.BlockSpec(shape, lambda i: (0, 0))
    return pl.pallas_call(
        functools.partial(_ffn_kernel, per_seq=per_seq, final_norm=final_norm),
        grid=(n // tm,),
        in_specs=[
            pl.BlockSpec((tm, d), lambda i: (i, 0)),
            pl.BlockSpec((SUBLANES, d), lambda i: (jnp.maximum(i * tb - 1, 0), 0)),
            pl.BlockSpec((SUBLANES, d), lambda i: (jnp.minimum((i + 1) * tb, nb - 1), 0)),
            const((1, d)),
            const((d, 2 * D_FF)),
            const((3, D_FF)),
            const((1, D_FF)),
            const((D_FF, d)),
            const((1, d)),
        ],
        out_specs=pl.BlockSpec((tm, d), lambda i: (i, 0)),
        out_shape=jax.ShapeDtypeStruct((n, d), F32),
        compiler_params=_params("arbitrary"),
        name="ffn",
    )(x, x, x, g.reshape(1, d), w_up, conv_w, conv_b.reshape(1, D_FF), w_dn, g_final.reshape(1, d))


def kernel(x, mem, norm_mix_g, w_in, ml_conv_w, ml_conv_b, ml_gate_b, ml_norm_g, hg_lb, hg_norm_g,
           w_branch_m, w_branch_h, w_out, norm_x_g, norm_mem_g, w_q_x, w_kv_x, w_o_x, norm_ffn_g,
           w_up, ffn_conv_w, ffn_conv_b, w_down, norm_final_g):
    bsz, seq, d = x.shape
    n_mem = mem.shape[1]
    depth = w_in.shape[0]
    n = bsz * seq
    xf = x.reshape(n, d)
    memf = mem.reshape(bsz * n_mem, d)
    lb_all = jnp.cumsum(jax.nn.softmax(hg_lb.astype(F32), axis=1), axis=1)

    for l in range(depth):
        w = w_in[l]
        wg = w[:, 4096:4096 + 4 * ML_HEADS].reshape(d, 4, ML_HEADS).transpose(0, 2, 1)
        wg = jnp.pad(wg, ((0, 0), (0, 0), (0, LANES - 4))).reshape(d, ML_HEADS * LANES)
        w_all = jnp.concatenate([w[:, :4096], w[:, 4096 + 4 * ML_HEADS:], wg], axis=1).astype(BF16)
        gate_b = jnp.pad(ml_gate_b[l].T, ((0, 0), (0, LANES - 4))).reshape(1, ML_HEADS * LANES)

        proj = _norm_matmul(xf, norm_mix_g[l], w_all, F32, tm=2048, tn=512)
        y_m = _mlstm(proj, ml_conv_w[l], ml_conv_b[l].reshape(1, -1), gate_b,
                     ml_norm_g[l].reshape(1, -1), bsz, seq)
        y_h = _hgrn(proj, lb_all[:, l], hg_norm_g[l].reshape(1, -1), bsz, seq)
        xf = _merge(xf, y_m, y_h, proj, w_branch_m[l].astype(BF16), w_branch_h[l].astype(BF16),
                    w_out[l].astype(BF16), tm=512)

        kv = _norm_matmul(memf, norm_mem_g[l], w_kv_x[l].astype(BF16), BF16, tm=2048, tn=512)
        xf = _xattn(xf, norm_x_g[l], w_q_x[l].astype(BF16), kv, w_o_x[l].astype(BF16), seq, n_mem, tm=512)

        xf = _ffn(xf, norm_ffn_g[l], w_up[l].astype(BF16), ffn_conv_w[l], ffn_conv_b[l],
                  w_down[l].astype(BF16), norm_final_g, l == depth - 1, seq, tm=512)
    return xf.reshape(bsz, seq, d)
```

```python
import functools

import jax
import jax.numpy as jnp
from jax import lax
from jax.experimental import pallas as pl
from jax.experimental.pallas import tpu as pltpu

F32 = jnp.float32
BF16 = jnp.bfloat16

EPS = 1e-6
D_MODEL = 1024
ML_HEADS = 4
ML_HEAD_DIM = 256
ML_CHUNK = 128
HG_HEADS = 8
HG_DIM = 128
HG_CHUNK = 128
X_HEADS = 4
X_HEAD_DIM = 256
D_FF = 2816
FF_CHUNK = 1408
SCAN_UNROLL = 4

V7X_VMEM_LIMIT_BYTES = 56 * 1024 * 1024
SUBLANES = 8
LANES = 128

COL_ML = 0
COL_HG = COL_ML + 4 * 1024
COL_GATE = COL_HG + 5 * 1024
COL_MLG = COL_GATE + 2 * 1024
PROJ_COLS = COL_MLG + ML_HEADS * LANES


def _params(*sem):
    return pltpu.CompilerParams(dimension_semantics=sem, vmem_limit_bytes=V7X_VMEM_LIMIT_BYTES)


def _sigmoid(x):
    return 0.5 * jnp.tanh(0.5 * x) + 0.5


def _log_sigmoid(x):
    return jnp.minimum(x, 0.0) - jnp.log(1.0 + jnp.exp(-jnp.abs(x)))


def _rms(x, g):
    return x * lax.rsqrt(jnp.mean(x * x, axis=-1, keepdims=True) + EPS) * g


def _dot(a, b):
    return jnp.dot(a, b, preferred_element_type=F32)


def _dot_nt(a, b):
    return lax.dot_general(a, b, (((1,), (1,)), ((), ())), preferred_element_type=F32)


def _dot_tn(a, b):
    return lax.dot_general(a, b, (((0,), (0,)), ((), ())), preferred_element_type=F32)


def _run_scans(nc, load, compute, store, state_refs):
    half = nc // 2

    def loop(lo, hi, finalize):
        def body(i, carry):
            steps = [(i * SCAN_UNROLL + j, rev) for j in range(SCAN_UNROLL) for rev in (False, True)]
            chunks = [(nc - 1 - s) if rev else s for s, rev in steps]
            loaded = [load(c, rev, finalize) for c, (_, rev) in zip(chunks, steps)]
            states = [[r[...] for r in refs] for refs in state_refs]
            gens = [compute(vals, rev, finalize, states) for vals, (_, rev) in zip(loaded, steps)]
            outs = [None] * len(gens)
            live = list(range(len(gens)))
            while live:
                for k in list(live):
                    try:
                        next(gens[k])
                    except StopIteration as done:
                        outs[k] = done.value
                        live.remove(k)
            for c, out in zip(chunks, outs):
                store(c, out)
            for refs, vals in zip(state_refs, states):
                for r, v in zip(refs, vals):
                    r[...] = v
            return carry

        lax.fori_loop(lo // SCAN_UNROLL, hi // SCAN_UNROLL, body, 0)

    loop(0, half, False)
    loop(half, nc, True)


def _norm_matmul_kernel(x_ref, g_ref, w_ref, o_ref, z_ref):
    @pl.when(pl.program_id(1) == 0)
    def _():
        z_ref[...] = _rms(x_ref[...], g_ref[...]).astype(BF16)

    o_ref[...] = _dot(z_ref[...], w_ref[...]).astype(o_ref.dtype)


def _norm_matmul(x, g, w, out_dtype, tm, tn):
    n, k = x.shape
    m = w.shape[1]
    tm = min(tm, n)
    tn = min(tn, m)
    return pl.pallas_call(
        _norm_matmul_kernel,
        grid=(n // tm, m // tn),
        in_specs=[
            pl.BlockSpec((tm, k), lambda i, j: (i, 0)),
            pl.BlockSpec((1, k), lambda i, j: (0, 0)),
            pl.BlockSpec((k, tn), lambda i, j: (0, j)),
        ],
        out_specs=pl.BlockSpec((tm, tn), lambda i, j: (i, j)),
        out_shape=jax.ShapeDtypeStruct((n, m), out_dtype),
        scratch_shapes=[pltpu.VMEM((tm, k), BF16)],
        compiler_params=_params("arbitrary", "arbitrary"),
        name="norm_matmul",
    )(x, g.reshape(1, k), w)


def _load_with_halo(ref, c):
    l = ML_CHUNK
    r0 = pl.multiple_of(c * l, l)
    before = pl.multiple_of(jnp.maximum(r0 - SUBLANES, 0), SUBLANES)
    after = pl.multiple_of(jnp.minimum(r0 + l, ref.shape[0] - SUBLANES), SUBLANES)
    return ref[pl.ds(r0, l), :], ref[pl.ds(before, SUBLANES), :], ref[pl.ds(after, SUBLANES), :]


def _conv_silu(halo, cw, cb, has_prev, has_next):
    cur, prev8, next8 = halo
    l = cur.shape[0]
    prev_row = jnp.where(has_prev, prev8[SUBLANES - 1:SUBLANES, :], 0.0)
    next_row = jnp.where(has_next, next8[0:1, :], 0.0)
    row = lax.broadcasted_iota(jnp.int32, cur.shape, 0)
    x_prev = jnp.where(row == 0, prev_row, pltpu.roll(cur, 1, axis=0))
    x_next = jnp.where(row == l - 1, next_row, pltpu.roll(cur, l - 1, axis=0))
    a = cw[0:1, :] * x_prev + cw[1:2, :] * cur + cw[2:3, :] * x_next + cb
    return a * _sigmoid(a)


def _mlstm_chunk(vals, rev, finalize, states, params, tri3t):
    l = ML_CHUNK
    d = int(rev)
    cwq, cwk, cbq, cbk, gate_b, norm_g = params
    if finalize:
        kb, q_t, k_t, v, g_rows = vals[:5]
    else:
        q_halo, k_halo, v, g, has_prev, has_next = vals
        qc = _conv_silu(q_halo, cwq, cbq, has_prev, has_next)
        kc = _conv_silu(k_halo, cwk, cbk, has_prev, has_next) * (ML_HEAD_DIM ** -0.5)
        kb, q_t, k_t = kc.astype(BF16), qc.T.astype(BF16), kc.T.astype(BF16)
        g_rows = (g + gate_b).T[0:SUBLANES, :]
        yield
    vb = v.astype(BF16)
    ic, fc = (2, 3) if rev else (0, 1)

    i_row = g_rows[ic:ic + 1, :]
    ls16 = jnp.concatenate([_log_sigmoid(g_rows), jnp.zeros_like(g_rows)], axis=0)
    bcum16 = _dot(_split3(ls16, axis=1), tri3t)
    kq = _dot(kb, q_t)
    yield
    bcum_row = bcum16[fc:fc + 1, :]
    g_end = bcum_row[:, 0:1] if rev else bcum_row[:, l - 1:l]
    a_row = i_row - bcum_row
    rid = lax.broadcasted_iota(jnp.int32, (2 * SUBLANES, l), 0)
    a3, b3 = _pieces3(a_row), _pieces3(bcum_row)
    lhs = jnp.where(rid == 0, a3[0], jnp.where(rid == 1, a3[1], jnp.where(rid == 2, a3[2],
                    jnp.where(rid < 6, 1.0, 0.0))))
    rhs = jnp.where(rid == 3, b3[0], jnp.where(rid == 4, b3[1], jnp.where(rid == 5, b3[2],
                    jnp.where(rid < 3, 1.0, 0.0))))
    s_idx = lax.broadcasted_iota(jnp.int32, (l, l), 0)
    t_idx = lax.broadcasted_iota(jnp.int32, (l, l), 1)
    pair = (s_idx >= t_idx) if rev else (s_idx <= t_idx)
    outer = _dot_tn(lhs.astype(BF16), rhs.astype(BF16))
    yield
    m_st = states[d][2]
    m_prev = m_st[0:1, 0:1]
    log_d = jnp.where(pair, outer, -jnp.inf)
    m_inter = bcum_row + m_prev
    m_t = jnp.maximum(m_inter, jnp.max(log_d, axis=0, keepdims=True))
    dmat = jnp.exp(log_d - m_t)
    w_inter = jnp.exp(m_inter - m_t)
    log_w = g_end + a_row
    m_new = jnp.maximum(g_end + m_prev, jnp.max(log_w, axis=1, keepdims=True))
    wgt = jnp.exp(log_w - m_new)
    decay = jnp.exp(g_end + m_prev - m_new)
    states[d][2] = jnp.broadcast_to(m_new, m_st.shape)
    yield

    c_st, n_st = states[d][0], states[d][1]
    c_b = c_st.astype(BF16)
    q_n = _dot(n_st.astype(BF16), q_t)[0:1, :]
    wgt16 = jnp.where(rid == 0, wgt, 0.0).astype(BF16)
    states[d][0] = decay * c_st + _dot((k_t.astype(F32) * wgt).astype(BF16), vb)
    states[d][1] = decay * n_st + _dot(wgt16, kb)
    yield

    sc = kq * dmat
    den = jnp.sum(sc, axis=0, keepdims=True) + w_inter * q_n
    scale = 1.0 / jnp.maximum(jnp.abs(den), jnp.exp(-m_t))
    lhs_t = jnp.concatenate([(sc * scale).astype(BF16),
                             (q_t.astype(F32) * (w_inter * scale)).astype(BF16)], axis=0)
    hc = _dot_tn(lhs_t, jnp.concatenate([vb, c_b], axis=0))
    if not finalize:
        return hc, kb, q_t, k_t, g_rows
    yield
    prev, og = vals[5:]
    hs = prev + hc
    y = hs * lax.rsqrt(jnp.mean(hs * hs, axis=-1, keepdims=True) + EPS) * norm_g
    return (y * _sigmoid(og),)


def _mlstm_kernel(q_ref, k_ref, v_ref, og_ref, g_ref, cwq_ref, cwk_ref, cbq_ref, cbk_ref,
                  gb_ref, ng_ref, out_ref, ks_ref, qt_ref, kt_ref, gt_ref, tri_ref, c_ref, n_ref, m_ref):
    l = ML_CHUNK
    nc = q_ref.shape[0] // l

    def rows_of(c):
        return pl.ds(pl.multiple_of(c * l, l), l)

    c_ref[...] = jnp.zeros_like(c_ref)
    n_ref[...] = jnp.zeros_like(n_ref)
    m_ref[...] = jnp.zeros_like(m_ref)
    tri_ref[0] = _tri3t(l, False)
    tri_ref[1] = _tri3t(l, True)

    def load(c, rev, finalize):
        rows = rows_of(c)
        if finalize:
            return (ks_ref[rows, :], qt_ref[:, rows], kt_ref[:, rows], v_ref[rows, :], gt_ref[:, rows],
                    out_ref[rows, :], og_ref[rows, :])
        return (_load_with_halo(q_ref, c), _load_with_halo(k_ref, c), v_ref[rows, :], g_ref[rows, :],
                c > 0, c < nc - 1)

    def compute(vals, rev, finalize, states):
        params = (cwq_ref[...], cwk_ref[...], cbq_ref[...], cbk_ref[...], gb_ref[...], ng_ref[...])
        return _mlstm_chunk(vals, rev, finalize, states, params, tri_ref[int(rev)])

    def store(c, out):
        rows = rows_of(c)
        out_ref[rows, :] = out[0]
        if len(out) > 1:
            ks_ref[rows, :], qt_ref[:, rows], kt_ref[:, rows], gt_ref[:, rows] = out[1:]

    _run_scans(nc, load, compute, store, [(c_ref.at[d], n_ref.at[d], m_ref.at[d]) for d in (0, 1)])


def _mlstm(proj, conv_w, conv_b, gate_b, norm_g, bsz, seq):
    n = proj.shape[0]
    hd = ML_HEAD_DIM
    blk = lambda off: pl.BlockSpec((seq, hd), lambda b, h, off=off: (b, off + h))
    wblk = lambda rows, off: pl.BlockSpec((rows, hd), lambda b, h, off=off: (0, off + h))
    return pl.pallas_call(
        _mlstm_kernel,
        grid=(bsz, ML_HEADS),
        in_specs=[
            blk(0), blk(ML_HEADS), blk(2 * ML_HEADS), blk(3 * ML_HEADS),
            pl.BlockSpec((seq, LANES), lambda b, h: (b, COL_MLG // LANES + h)),
            wblk(3, 0), wblk(3, ML_HEADS), wblk(1, 0), wblk(1, ML_HEADS),
            pl.BlockSpec((1, LANES), lambda b, h: (0, h)),
            wblk(1, 0),
        ],
        out_specs=pl.BlockSpec((seq, hd), lambda b, h: (b, h)),
        out_shape=jax.ShapeDtypeStruct((n, ML_HEADS * hd), F32),
        scratch_shapes=[pltpu.VMEM((seq, hd), BF16), pltpu.VMEM((hd, seq), BF16), pltpu.VMEM((hd, seq), BF16),
                        pltpu.VMEM((SUBLANES, seq), F32), pltpu.VMEM((2, 3 * ML_CHUNK, ML_CHUNK), BF16),
                        pltpu.VMEM((2, hd, hd), F32), pltpu.VMEM((2, 2 * SUBLANES, hd), F32),
                        pltpu.VMEM((2, 1, LANES), F32)],
        compiler_params=_params("arbitrary", "arbitrary"),
        name="mlstm",
    )(proj, proj, proj, proj, proj, conv_w, conv_w, conv_b, conv_b, gate_b, norm_g)


def _pieces3(x):
    hi = x.astype(BF16).astype(F32)
    r1 = x - hi
    mid = r1.astype(BF16).astype(F32)
    return hi, mid, r1 - mid


def _split3(x, axis=0):
    return jnp.concatenate([p.astype(BF16) for p in _pieces3(x)], axis=axis)


def _tri3(l, rev):
    t = lax.broadcasted_iota(jnp.int32, (l, 3 * l), 0)
    u = lax.broadcasted_iota(jnp.int32, (l, 3 * l), 1) & (l - 1)
    inside = (u >= t) if rev else (u <= t)
    return jnp.where(inside, 1.0, 0.0).astype(BF16)


def _tri3t(l, rev):
    u = lax.broadcasted_iota(jnp.int32, (3 * l, l), 0) & (l - 1)
    t = lax.broadcasted_iota(jnp.int32, (3 * l, l), 1)
    inside = (u >= t) if rev else (u <= t)
    return jnp.where(inside, 1.0, 0.0).astype(BF16)


def _cumsum_rows(x, tri3):
    return _dot(tri3, _split3(x))


def _level_factors(b, f, q, kk, w, rev, row):
    l = b.shape[0]
    late = (row & w) != 0
    q_side = (~late) if rev else late
    if w == 1:
        return jnp.where(q_side, q * f, kk)
    if w == 2:
        p = row & 3
        f_prev = pltpu.roll(f, 1, axis=0)
        f_next = pltpu.roll(f, l - 1, axis=0)
        if rev:
            e = jnp.where(p == 0, f * f_next, jnp.where(p == 1, f, jnp.where(p == 3, f_prev, 1.0)))
        else:
            e = jnp.where(p == 3, f * f_prev, jnp.where(p == 2, f, jnp.where(p == 0, f_next, 1.0)))
        return jnp.where(q_side, q, kk) * e
    if w == 4:
        pieces = []
        for start in range(0, l, 2 * w):
            r = start + w if rev else start + w - 1
            pieces.append(jnp.broadcast_to(b[r:r + 1, :], (2 * w, b.shape[1])))
        return jnp.where(q_side, q, kk) * jnp.exp(-jnp.abs(b - jnp.concatenate(pieces, axis=0)))
    base, expo = [], []
    for start in range(0, l, 2 * w):
        lo, hi = slice(start, start + w), slice(start + w, start + 2 * w)
        if rev:
            r = b[start + w:start + w + 1, :]
            base += [q[lo], kk[hi]]
            expo += [b[lo] - r, r - b[hi]]
        else:
            r = b[start + w - 1:start + w, :]
            base += [kk[lo], q[hi]]
            expo += [r - b[lo], b[hi] - r]
    return jnp.concatenate(base, axis=0) * jnp.exp(jnp.concatenate(expo, axis=0))


def _hgrn_chunk(vals, rev, finalize, states, lb, norm_g, tri3):
    l = HG_CHUNK
    d = int(rev)
    qr, fr, v = vals[:3]
    f = lb + (1.0 - lb) * _sigmoid(fr)
    kk = 1.0 - f
    q = qr * _sigmoid(qr)
    vb = v.astype(BF16)
    b = _cumsum_rows(jnp.log(f), tri3)
    yield

    st = states[d][0]
    o_inter = _dot_nt((q * jnp.exp(b)).astype(BF16), st.astype(BF16))
    b_end = b[0:1, :] if rev else b[l - 1:l, :]
    kd = (kk * jnp.exp(b_end - b)).astype(BF16)
    states[d][0] = jnp.exp(b_end) * st + _dot_tn(vb, kd)


    row = lax.broadcasted_iota(jnp.int32, (l, HG_DIM), 0)
    t_idx = lax.broadcasted_iota(jnp.int32, (l, l), 0)
    s_idx = lax.broadcasted_iota(jnp.int32, (l, l), 1)
    diff = t_idx ^ s_idx
    att = None
    w = l // 2
    while w >= 1:
        mb = _level_factors(b, f, q, kk, w, rev, row).astype(BF16)
        a = _dot_nt(mb, mb)
        att = a if att is None else jnp.where(diff < 2 * w, a, att)
        w //= 2
    yield

    keep = (s_idx > t_idx) if rev else (s_idx < t_idx)
    att = jnp.where(diff == 0, jnp.sum(q * kk, axis=1, keepdims=True), jnp.where(keep, att, 0.0))
    o = _dot(att.astype(BF16), vb) + o_inter
    if finalize:
        yield
        prev, gg = vals[3:]
        hs = prev + o
        y = hs * lax.rsqrt(jnp.mean(hs * hs, axis=-1, keepdims=True) + EPS) * norm_g
        o = y * (gg * _sigmoid(gg))
    return o


def _hgrn_kernel(q_ref, ff_ref, fb_ref, v_ref, gg_ref, lb_ref, ng_ref, out_ref, st_ref, tri_ref):
    l = HG_CHUNK
    nc = q_ref.shape[0] // l
    st_ref[...] = jnp.zeros_like(st_ref)
    tri_ref[0] = _tri3(l, False)
    tri_ref[1] = _tri3(l, True)

    def rows_of(c):
        return pl.ds(pl.multiple_of(c * l, l), l)

    def load(c, rev, finalize):
        rows = rows_of(c)
        vals = (q_ref[rows, :], (fb_ref if rev else ff_ref)[rows, :], v_ref[rows, :])
        if finalize:
            vals += (out_ref[rows, :], gg_ref[rows, :])
        return vals

    def compute(vals, rev, finalize, states):
        d = int(rev)
        return _hgrn_chunk(vals, rev, finalize, states, lb_ref[d:d + 1, :], ng_ref[...], tri_ref[d])

    def store(c, out):
        out_ref[rows_of(c), :] = out

    _run_scans(nc, load, compute, store, [(st_ref.at[0],), (st_ref.at[1],)])


def _hgrn(proj, lb, norm_g, bsz, seq):
    n = proj.shape[0]
    base = COL_HG // HG_DIM
    blk = lambda grp: pl.BlockSpec((seq, HG_DIM), lambda b, h, grp=grp: (b, base + grp * HG_HEADS + h))
    return pl.pallas_call(
        _hgrn_kernel,
        grid=(bsz, HG_HEADS),
        in_specs=[
            blk(0), blk(1), blk(2), blk(3), blk(4),
            pl.BlockSpec((2, HG_DIM), lambda b, h: (0, h)),
            pl.BlockSpec((1, HG_DIM), lambda b, h: (0, h)),
        ],
        out_specs=pl.BlockSpec((seq, HG_DIM), lambda b, h: (b, h)),
        out_shape=jax.ShapeDtypeStruct((n, HG_HEADS * HG_DIM), F32),
        scratch_shapes=[pltpu.VMEM((2, HG_DIM, HG_DIM), F32), pltpu.VMEM((2, HG_CHUNK, 3 * HG_CHUNK), BF16)],
        compiler_params=_params("arbitrary", "arbitrary"),
        name="hgrn2",
    )(proj, proj, proj, proj, proj, lb, norm_g)


def _merge_kernel(x_ref, ym_ref, yh_ref, gm_ref, gh_ref, wm_ref, wh_ref, wo_ref, o_ref):
    pm = _dot(ym_ref[...].astype(BF16), wm_ref[...])
    ph = _dot(yh_ref[...].astype(BF16), wh_ref[...])
    merged = _sigmoid(gm_ref[...]) * pm + _sigmoid(gh_ref[...]) * ph
    o_ref[...] = x_ref[...] + _dot(merged.astype(BF16), wo_ref[...])


def _merge(x, y_m, y_h, proj, w_m, w_h, w_o, tm):
    n, d = x.shape
    tm = min(tm, n)
    row = lambda cb: pl.BlockSpec((tm, d), lambda i, cb=cb: (i, cb))
    full = pl.BlockSpec((d, d), lambda i: (0, 0))
    return pl.pallas_call(
        _merge_kernel,
        grid=(n // tm,),
        in_specs=[row(0), row(0), row(0), row(COL_GATE // d), row(COL_GATE // d + 1), full, full, full],
        out_specs=row(0),
        out_shape=jax.ShapeDtypeStruct((n, d), F32),
        compiler_params=_params("arbitrary"),
        name="merge",
    )(x, y_m, y_h, proj, proj, w_m, w_h, w_o)


def _xattn_kernel(x_ref, g_ref, wq_ref, kv_ref, wo_ref, o_ref):
    x = x_ref[...]
    u = _rms(x, g_ref[...]).astype(BF16)
    q = _dot(u, wq_ref[...])
    heads = []
    for h in range(X_HEADS):
        lo = h * X_HEAD_DIM
        qh = q[:, lo:lo + X_HEAD_DIM].astype(BF16)
        kh = kv_ref[:, lo:lo + X_HEAD_DIM]
        vh = kv_ref[:, D_MODEL + lo:D_MODEL + lo + X_HEAD_DIM]
        s = _dot_nt(qh, kh) * (X_HEAD_DIM ** -0.5)
        p = jnp.exp(s - jnp.max(s, axis=-1, keepdims=True))
        p = p * (1.0 / jnp.sum(p, axis=-1, keepdims=True))
        heads.append(_dot(p.astype(BF16), vh).astype(BF16))
    att = jnp.concatenate(heads, axis=-1)
    o_ref[...] = x + _dot(att, wo_ref[...])


def _xattn(x, g, w_q, kv, w_o, seq, n_mem, tm):
    n, d = x.shape
    tm = min(tm, seq)
    per_seq = seq // tm
    full = pl.BlockSpec((d, d), lambda i: (0, 0))
    return pl.pallas_call(
        _xattn_kernel,
        grid=(n // tm,),
        in_specs=[
            pl.BlockSpec((tm, d), lambda i: (i, 0)),
            pl.BlockSpec((1, d), lambda i: (0, 0)),
            full,
            pl.BlockSpec((n_mem, 2 * d), lambda i: (i // per_seq, 0)),
            full,
        ],
        out_specs=pl.BlockSpec((tm, d), lambda i: (i, 0)),
        out_shape=jax.ShapeDtypeStruct((n, d), F32),
        compiler_params=_params("arbitrary"),
        name="xattn",
    )(x, g.reshape(1, d), w_q, kv, w_o)


def _gelu(x):
    return 0.5 * x * (1.0 + jnp.tanh(0.7978845608028654 * (x + 0.044715 * (x * x * x))))


def _ffn_kernel(x_ref, xp_ref, xn_ref, g_ref, wup_ref, cw_ref, cb_ref, wdn_ref, gf_ref, o_ref, *,
                per_seq, final_norm):
    i = pl.program_id(0)
    tm = x_ref.shape[0]
    x = x_ref[...]
    g = g_ref[...]
    first = (i % per_seq) == 0
    last = (i % per_seq) == per_seq - 1
    up_prev = jnp.where(first, 0.0, _rms(xp_ref[...], g))
    up_next = jnp.where(last, 0.0, _rms(xn_ref[...], g))
    u_ext = jnp.concatenate([up_prev, _rms(x, g), up_next], axis=0).astype(BF16)
    ext = tm + 2 * SUBLANES
    acc = x
    for c0 in range(0, D_FF, FF_CHUNK):
        a_ext = _dot(u_ext, wup_ref[:, c0:c0 + FF_CHUNK])
        gate = _dot(u_ext, wup_ref[:, D_FF + c0:D_FF + c0 + FF_CHUNK])[SUBLANES:SUBLANES + tm, :]
        a_prev = pltpu.roll(a_ext, 1, axis=0)[SUBLANES:SUBLANES + tm, :]
        a_next = pltpu.roll(a_ext, ext - 1, axis=0)[SUBLANES:SUBLANES + tm, :]
        a_cur = a_ext[SUBLANES:SUBLANES + tm, :]
        a = (cw_ref[0:1, c0:c0 + FF_CHUNK] * a_prev + cw_ref[1:2, c0:c0 + FF_CHUNK] * a_cur
             + cw_ref[2:3, c0:c0 + FF_CHUNK] * a_next + cb_ref[:, c0:c0 + FF_CHUNK])
        acc = acc + _dot((_gelu(a) * gate).astype(BF16), wdn_ref[c0:c0 + FF_CHUNK, :])
    o_ref[...] = _rms(acc, gf_ref[...]) if final_norm else acc


def _ffn(x, g, w_up, conv_w, conv_b, w_dn, g_final, final_norm, seq, tm):
    n, d = x.shape
    tm = min(tm, seq)
    per_seq = seq // tm
    tb = tm // SUBLANES
    nb = n // SUBLANES
    const = lambda shape: pl.BlockSpec(shape, lambda i: (0, 0))
    return pl.pallas_call(
        functools.partial(_ffn_kernel, per_seq=per_seq, final_norm=final_norm),
        grid=(n // tm,),
        in_specs=[
            pl.BlockSpec((tm, d), lambda i: (i, 0)),
            pl.BlockSpec((SUBLANES, d), lambda i: (jnp.maximum(i * tb - 1, 0), 0)),
            pl.BlockSpec((SUBLANES, d), lambda i: (jnp.minimum((i + 1) * tb, nb - 1), 0)),
            const((1, d)),
            const((d, 2 * D_FF)),
            const((3, D_FF)),
            const((1, D_FF)),
            const((D_FF, d)),
            const((1, d)),
        ],
        out_specs=pl.BlockSpec((tm, d), lambda i: (i, 0)),
        out_shape=jax.ShapeDtypeStruct((n, d), F32),
        compiler_params=_params("arbitrary"),
        name="ffn",
    )(x, x, x, g.reshape(1, d), w_up, conv_w, conv_b.reshape(1, D_FF), w_dn, g_final.reshape(1, d))


def kernel(x, mem, norm_mix_g, w_in, ml_conv_w, ml_conv_b, ml_gate_b, ml_norm_g, hg_lb, hg_norm_g,
           w_branch_m, w_branch_h, w_out, norm_x_g, norm_mem_g, w_q_x, w_kv_x, w_o_x, norm_ffn_g,
           w_up, ffn_conv_w, ffn_conv_b, w_down, norm_final_g):
    bsz, seq, d = x.shape
    n_mem = mem.shape[1]
    depth = w_in.shape[0]
    n = bsz * seq
    xf = x.reshape(n, d)
    memf = mem.reshape(bsz * n_mem, d)
    lb_all = jnp.cumsum(jax.nn.softmax(hg_lb.astype(F32), axis=1), axis=1)

    for l in range(depth):
        w = w_in[l]
        wg = w[:, 4096:4096 + 4 * ML_HEADS].reshape(d, 4, ML_HEADS).transpose(0, 2, 1)
        wg = jnp.pad(wg, ((0, 0), (0, 0), (0, LANES - 4))).reshape(d, ML_HEADS * LANES)
        w_all = jnp.concatenate([w[:, :4096], w[:, 4096 + 4 * ML_HEADS:], wg], axis=1).astype(BF16)
        gate_b = jnp.pad(ml_gate_b[l].T, ((0, 0), (0, LANES - 4))).reshape(1, ML_HEADS * LANES)

        proj = _norm_matmul(xf, norm_mix_g[l], w_all, F32, tm=2048, tn=512)
        y_m = _mlstm(proj, ml_conv_w[l], ml_conv_b[l].reshape(1, -1), gate_b,
                     ml_norm_g[l].reshape(1, -1), bsz, seq)
        y_h = _hgrn(proj, lb_all[:, l], hg_norm_g[l].reshape(1, -1), bsz, seq)
        xf = _merge(xf, y_m, y_h, proj, w_branch_m[l].astype(BF16), w_branch_h[l].astype(BF16),
                    w_out[l].astype(BF16), tm=512)

        kv = _norm_matmul(memf, norm_mem_g[l], w_kv_x[l].astype(BF16), BF16, tm=2048, tn=512)
        xf = _xattn(xf, norm_x_g[l], w_q_x[l].astype(BF16), kv, w_o_x[l].astype(BF16), seq, n_mem, tm=512)

        xf = _ffn(xf, norm_ffn_g[l], w_up[l].astype(BF16), ffn_conv_w[l], ffn_conv_b[l],
                  w_down[l].astype(BF16), norm_final_g, l == depth - 1, seq, tm=512)
    return xf.reshape(bsz, seq, d)
```

```python
import functools

import jax
import jax.numpy as jnp
from jax import lax
from jax.experimental import pallas as pl
from jax.experimental.pallas import tpu as pltpu

F32 = jnp.float32
BF16 = jnp.bfloat16

EPS = 1e-6
D_MODEL = 1024
ML_HEADS = 4
ML_HEAD_DIM = 256
ML_CHUNK = 128
HG_HEADS = 8
HG_DIM = 128
HG_CHUNK = 128
X_HEADS = 4
X_HEAD_DIM = 256
D_FF = 2816
FF_CHUNK = 2816
SCAN_UNROLL = 4
TM_PROJ, TN_PROJ = 2048, 512
TM_MERGE = 1024
TM_XATTN = 1024
TM_FFN = 1024

V7X_VMEM_LIMIT_BYTES = 56 * 1024 * 1024
SUBLANES = 8
BF16_ROWS = 16
LANES = 128

COL16_ML = 0
COL16_HG = 4 * 1024
COL16_GATE = 7 * 1024
COLS16 = 9 * 1024
COL32_HGF = 0
COL32_MLG = 2 * 1024
COLS32 = COL32_MLG + ML_HEADS * LANES


def _params(*sem):
    return pltpu.CompilerParams(dimension_semantics=sem, vmem_limit_bytes=V7X_VMEM_LIMIT_BYTES)


def _sigmoid(x):
    return 0.5 * jnp.tanh(0.5 * x) + 0.5


def _log_sigmoid(x):
    return jnp.minimum(x, 0.0) - jnp.log(1.0 + jnp.exp(-jnp.abs(x)))


def _rms(x, g):
    return x * lax.rsqrt(jnp.mean(x * x, axis=-1, keepdims=True) + EPS) * g


def _dot(a, b):
    return jnp.dot(a, b, preferred_element_type=F32)


def _dot_nt(a, b):
    return lax.dot_general(a, b, (((1,), (1,)), ((), ())), preferred_element_type=F32)


def _dot_tn(a, b):
    return lax.dot_general(a, b, (((0,), (0,)), ((), ())), preferred_element_type=F32)


def _run_scans(nc, load, compute, store, state_refs):
    half = nc // 2

    def loop(lo, hi, finalize):
        def body(i, carry):
            steps = [(i * SCAN_UNROLL + j, rev) for j in range(SCAN_UNROLL) for rev in (False, True)]
            chunks = [(nc - 1 - s) if rev else s for s, rev in steps]
            loaded = [load(c, rev, finalize) for c, (_, rev) in zip(chunks, steps)]
            states = [[r[...] for r in refs] for refs in state_refs]
            gens = [compute(vals, rev, finalize, states) for vals, (_, rev) in zip(loaded, steps)]
            outs = [None] * len(gens)
            live = list(range(len(gens)))
            while live:
                for k in list(live):
                    try:
                        next(gens[k])
                    except StopIteration as done:
                        outs[k] = done.value
                        live.remove(k)
            for c, out in zip(chunks, outs):
                store(c, out, finalize)
            for refs, vals in zip(state_refs, states):
                for r, v in zip(refs, vals):
                    r[...] = v
            return carry

        lax.fori_loop(lo // SCAN_UNROLL, hi // SCAN_UNROLL, body, 0)

    loop(0, half, False)
    loop(half, nc, True)


def _norm_matmul_kernel(x_ref, g_ref, w_ref, o_ref, z_ref):
    @pl.when(pl.program_id(1) == 0)
    def _():
        z_ref[...] = _rms(x_ref[...], g_ref[...]).astype(BF16)

    o_ref[...] = _dot(z_ref[...], w_ref[...]).astype(o_ref.dtype)


def _norm_matmul(x, g, w, out_dtype, tm, tn):
    n, k = x.shape
    m = w.shape[1]
    tm = min(tm, n)
    tn = min(tn, m)
    return pl.pallas_call(
        _norm_matmul_kernel,
        grid=(n // tm, m // tn),
        in_specs=[
            pl.BlockSpec((tm, k), lambda i, j: (i, 0)),
            pl.BlockSpec((1, k), lambda i, j: (0, 0)),
            pl.BlockSpec((k, tn), lambda i, j: (0, j)),
        ],
        out_specs=pl.BlockSpec((tm, tn), lambda i, j: (i, j)),
        out_shape=jax.ShapeDtypeStruct((n, m), out_dtype),
        scratch_shapes=[pltpu.VMEM((tm, k), BF16)],
        compiler_params=_params("arbitrary", "arbitrary"),
        name="norm_matmul",
    )(x, g.reshape(1, k), w)


def _load_with_halo(ref, c):
    l = ML_CHUNK
    r0 = pl.multiple_of(c * l, l)
    before = pl.multiple_of(jnp.maximum(r0 - BF16_ROWS, 0), BF16_ROWS)
    after = pl.multiple_of(jnp.minimum(r0 + l, ref.shape[0] - BF16_ROWS), BF16_ROWS)
    return ref[pl.ds(r0, l), :], ref[pl.ds(before, BF16_ROWS), :], ref[pl.ds(after, BF16_ROWS), :]


def _conv_silu(halo, cw, cb, has_prev, has_next):
    cur, before, after = (t.astype(F32) for t in halo)
    l = cur.shape[0]
    prev_row = jnp.where(has_prev, before[BF16_ROWS - 1:BF16_ROWS, :], 0.0)
    next_row = jnp.where(has_next, after[0:1, :], 0.0)
    row = lax.broadcasted_iota(jnp.int32, cur.shape, 0)
    x_prev = jnp.where(row == 0, prev_row, pltpu.roll(cur, 1, axis=0))
    x_next = jnp.where(row == l - 1, next_row, pltpu.roll(cur, l - 1, axis=0))
    a = cw[0:1, :] * x_prev + cw[1:2, :] * cur + cw[2:3, :] * x_next + cb
    return a * _sigmoid(a)


def _mlstm_chunk(vals, rev, finalize, states, params, tri3t):
    l = ML_CHUNK
    d = int(rev)
    cwq, cwk, cbq, cbk, gate_b, norm_g = params
    if finalize:
        kb, q_t, k_t, vb, g_rows = vals[:5]
    else:
        q_halo, k_halo, vb, g, has_prev, has_next = vals
        qc = _conv_silu(q_halo, cwq, cbq, has_prev, has_next)
        kc = _conv_silu(k_halo, cwk, cbk, has_prev, has_next) * (ML_HEAD_DIM ** -0.5)
        kb, q_t, k_t = kc.astype(BF16), qc.T.astype(BF16), kc.T.astype(BF16)
        g_rows = (g + gate_b).T[0:SUBLANES, :]
        yield
    ic, fc = (2, 3) if rev else (0, 1)

    i_row = g_rows[ic:ic + 1, :]
    ls16 = jnp.concatenate([_log_sigmoid(g_rows), jnp.zeros_like(g_rows)], axis=0)
    bcum16 = _dot(_split3(ls16, axis=1), tri3t)
    kq = _dot(kb, q_t)
    yield
    bcum_row = bcum16[fc:fc + 1, :]
    g_end = bcum_row[:, 0:1] if rev else bcum_row[:, l - 1:l]
    a_row = i_row - bcum_row
    rid = lax.broadcasted_iota(jnp.int32, (2 * SUBLANES, l), 0)
    a3, b3 = _pieces3(a_row), _pieces3(bcum_row)
    lhs = jnp.where(rid == 0, a3[0], jnp.where(rid == 1, a3[1], jnp.where(rid == 2, a3[2],
                    jnp.where(rid < 6, 1.0, 0.0))))
    rhs = jnp.where(rid == 3, b3[0], jnp.where(rid == 4, b3[1], jnp.where(rid == 5, b3[2],
                    jnp.where(rid < 3, 1.0, 0.0))))
    s_idx = lax.broadcasted_iota(jnp.int32, (l, l), 0)
    t_idx = lax.broadcasted_iota(jnp.int32, (l, l), 1)
    pair = (s_idx >= t_idx) if rev else (s_idx <= t_idx)
    outer = _dot_tn(lhs.astype(BF16), rhs.astype(BF16))
    yield
    m_st = states[d][2]
    m_prev = m_st[0:1, 0:1]
    log_d = jnp.where(pair, outer, -jnp.inf)
    m_inter = bcum_row + m_prev
    m_t = jnp.maximum(m_inter, jnp.max(log_d, axis=0, keepdims=True))
    dmat = jnp.exp(log_d - m_t)
    w_inter = jnp.exp(m_inter - m_t)
    log_w = g_end + a_row
    m_new = jnp.maximum(g_end + m_prev, jnp.max(log_w, axis=1, keepdims=True))
    wgt = jnp.exp(log_w - m_new)
    decay = jnp.exp(g_end + m_prev - m_new)
    states[d][2] = jnp.broadcast_to(m_new, m_st.shape)
    yield

    c_st, n_st = states[d][0], states[d][1]
    c_b = c_st.astype(BF16)
    q_n = _dot(n_st.astype(BF16), q_t)[0:1, :]
    wgt16 = jnp.where(rid == 0, wgt, 0.0).astype(BF16)
    states[d][0] = decay * c_st + _dot((k_t.astype(F32) * wgt).astype(BF16), vb)
    states[d][1] = decay * n_st + _dot(wgt16, kb)
    yield

    sc = kq * dmat
    den = jnp.sum(sc, axis=0, keepdims=True) + w_inter * q_n
    scale = 1.0 / jnp.maximum(jnp.abs(den), jnp.exp(-m_t))
    lhs_t = jnp.concatenate([(sc * scale).astype(BF16),
                             (q_t.astype(F32) * (w_inter * scale)).astype(BF16)], axis=0)
    hc = _dot_tn(lhs_t, jnp.concatenate([vb, c_b], axis=0))
    if not finalize:
        return hc, kb, q_t, k_t, g_rows
    yield
    prev, og = vals[5:]
    hs = prev + hc
    y = hs * lax.rsqrt(jnp.mean(hs * hs, axis=-1, keepdims=True) + EPS) * norm_g
    return ((y * _sigmoid(og.astype(F32))).astype(BF16),)


def _mlstm_kernel(q_ref, k_ref, v_ref, og_ref, g_ref, cwq_ref, cwk_ref, cbq_ref, cbk_ref, gb_ref, ng_ref,
                  out_ref, h_ref, ks_ref, qt_ref, kt_ref, gt_ref, tri_ref, c_ref, n_ref, m_ref):
    l = ML_CHUNK
    nc = q_ref.shape[0] // l

    def rows_of(c):
        return pl.ds(pl.multiple_of(c * l, l), l)

    c_ref[...] = jnp.zeros_like(c_ref)
    n_ref[...] = jnp.zeros_like(n_ref)
    m_ref[...] = jnp.zeros_like(m_ref)
    tri_ref[0] = _tri3t(l, False)
    tri_ref[1] = _tri3t(l, True)

    def load(c, rev, finalize):
        rows = rows_of(c)
        if finalize:
            return (ks_ref[rows, :], qt_ref[:, rows], kt_ref[:, rows], v_ref[rows, :], gt_ref[:, rows],
                    h_ref[rows, :], og_ref[rows, :])
        return (_load_with_halo(q_ref, c), _load_with_halo(k_ref, c), v_ref[rows, :], g_ref[rows, :],
                c > 0, c < nc - 1)

    def compute(vals, rev, finalize, states):
        params = (cwq_ref[...], cwk_ref[...], cbq_ref[...], cbk_ref[...], gb_ref[...], ng_ref[...])
        return _mlstm_chunk(vals, rev, finalize, states, params, tri_ref[int(rev)])

    def store(c, out, finalize):
        rows = rows_of(c)
        if finalize:
            out_ref[rows, :] = out[0]
        else:
            h_ref[rows, :], ks_ref[rows, :], qt_ref[:, rows], kt_ref[:, rows], gt_ref[:, rows] = out

    _run_scans(nc, load, compute, store, [(c_ref.at[d], n_ref.at[d], m_ref.at[d]) for d in (0, 1)])


def _mlstm(p16, p32, conv_w, conv_b, gate_b, norm_g, bsz, seq):
    n = p16.shape[0]
    hd = ML_HEAD_DIM
    blk = lambda off: pl.BlockSpec((seq, hd), lambda b, h, off=off: (b, COL16_ML // hd + off + h))
    wblk = lambda rows, off: pl.BlockSpec((rows, hd), lambda b, h, off=off: (0, off + h))
    return pl.pallas_call(
        _mlstm_kernel,
        grid=(bsz, ML_HEADS),
        in_specs=[
            blk(0), blk(ML_HEADS), blk(2 * ML_HEADS), blk(3 * ML_HEADS),
            pl.BlockSpec((seq, LANES), lambda b, h: (b, COL32_MLG // LANES + h)),
            wblk(3, 0), wblk(3, ML_HEADS), wblk(1, 0), wblk(1, ML_HEADS),
            pl.BlockSpec((1, LANES), lambda b, h: (0, h)),
            wblk(1, 0),
        ],
        out_specs=pl.BlockSpec((seq, hd), lambda b, h: (b, h)),
        out_shape=jax.ShapeDtypeStruct((n, ML_HEADS * hd), BF16),
        scratch_shapes=[pltpu.VMEM((seq, hd), F32),
                        pltpu.VMEM((seq, hd), BF16), pltpu.VMEM((hd, seq), BF16), pltpu.VMEM((hd, seq), BF16),
                        pltpu.VMEM((SUBLANES, seq), F32), pltpu.VMEM((2, 3 * ML_CHUNK, ML_CHUNK), BF16),
                        pltpu.VMEM((2, hd, hd), F32), pltpu.VMEM((2, 2 * SUBLANES, hd), F32),
                        pltpu.VMEM((2, 1, LANES), F32)],
        compiler_params=_params("arbitrary", "arbitrary"),
        name="mlstm",
    )(p16, p16, p16, p16, p32, conv_w, conv_w, conv_b, conv_b, gate_b, norm_g)


def _pieces3(x):
    hi = x.astype(BF16).astype(F32)
    r1 = x - hi
    mid = r1.astype(BF16).astype(F32)
    return hi, mid, r1 - mid


def _split3(x, axis=0):
    return jnp.concatenate([p.astype(BF16) for p in _pieces3(x)], axis=axis)


def _tri3(l, rev):
    t = lax.broadcasted_iota(jnp.int32, (l, 3 * l), 0)
    u = lax.broadcasted_iota(jnp.int32, (l, 3 * l), 1) & (l - 1)
    inside = (u >= t) if rev else (u <= t)
    return jnp.where(inside, 1.0, 0.0).astype(BF16)


def _tri3t(l, rev):
    u = lax.broadcasted_iota(jnp.int32, (3 * l, l), 0) & (l - 1)
    t = lax.broadcasted_iota(jnp.int32, (3 * l, l), 1)
    inside = (u >= t) if rev else (u <= t)
    return jnp.where(inside, 1.0, 0.0).astype(BF16)


def _cumsum_rows(x, tri3):
    return _dot(tri3, _split3(x))


def _level_factors(b, f, q, kk, w, rev, row):
    l = b.shape[0]
    late = (row & w) != 0
    q_side = (~late) if rev else late
    if w == 1:
        return jnp.where(q_side, q * f, kk)
    if w == 2:
        p = row & 3
        f_prev = pltpu.roll(f, 1, axis=0)
        f_next = pltpu.roll(f, l - 1, axis=0)
        if rev:
            e = jnp.where(p == 0, f * f_next, jnp.where(p == 1, f, jnp.where(p == 3, f_prev, 1.0)))
        else:
            e = jnp.where(p == 3, f * f_prev, jnp.where(p == 2, f, jnp.where(p == 0, f_next, 1.0)))
        return jnp.where(q_side, q, kk) * e
    if w == 4:
        pieces = []
        for start in range(0, l, 2 * w):
            r = start + w if rev else start + w - 1
            pieces.append(jnp.broadcast_to(b[r:r + 1, :], (2 * w, b.shape[1])))
        return jnp.where(q_side, q, kk) * jnp.exp(-jnp.abs(b - jnp.concatenate(pieces, axis=0)))
    base, expo = [], []
    for start in range(0, l, 2 * w):
        lo, hi = slice(start, start + w), slice(start + w, start + 2 * w)
        if rev:
            r = b[start + w:start + w + 1, :]
            base += [q[lo], kk[hi]]
            expo += [b[lo] - r, r - b[hi]]
        else:
            r = b[start + w - 1:start + w, :]
            base += [kk[lo], q[hi]]
            expo += [r - b[lo], b[hi] - r]
    return jnp.concatenate(base, axis=0) * jnp.exp(jnp.concatenate(expo, axis=0))


def _hgrn_chunk(vals, rev, finalize, states, lb, norm_g, tri3):
    l = HG_CHUNK
    d = int(rev)
    qr, fr, vb = vals[:3]
    qr = qr.astype(F32)
    f = lb + (1.0 - lb) * _sigmoid(fr)
    kk = 1.0 - f
    q = qr * _sigmoid(qr)
    b = _cumsum_rows(jnp.log(f), tri3)
    yield

    st = states[d][0]
    o_inter = _dot_nt((q * jnp.exp(b)).astype(BF16), st.astype(BF16))
    b_end = b[0:1, :] if rev else b[l - 1:l, :]
    kd = (kk * jnp.exp(b_end - b)).astype(BF16)
    states[d][0] = jnp.exp(b_end) * st + _dot_tn(vb, kd)


    row = lax.broadcasted_iota(jnp.int32, (l, HG_DIM), 0)
    t_idx = lax.broadcasted_iota(jnp.int32, (l, l), 0)
    s_idx = lax.broadcasted_iota(jnp.int32, (l, l), 1)
    diff = t_idx ^ s_idx
    att = None
    w = l // 2
    while w >= 1:
        mb = _level_factors(b, f, q, kk, w, rev, row).astype(BF16)
        a = _dot_nt(mb, mb)
        att = a if att is None else jnp.where(diff < 2 * w, a, att)
        w //= 2
    yield

    keep = (s_idx > t_idx) if rev else (s_idx < t_idx)
    att = jnp.where(diff == 0, jnp.sum(q * kk, axis=1, keepdims=True), jnp.where(keep, att, 0.0))
    o = _dot(att.astype(BF16), vb) + o_inter
    if finalize:
        yield
        prev, gg = vals[3:]
        gg = gg.astype(F32)
        hs = prev + o
        y = hs * lax.rsqrt(jnp.mean(hs * hs, axis=-1, keepdims=True) + EPS) * norm_g
        o = (y * (gg * _sigmoid(gg))).astype(BF16)
    return o


def _hgrn_kernel(q_ref, v_ref, gg_ref, ff_ref, fb_ref, lb_ref, ng_ref, out_ref, h_ref, st_ref, tri_ref):
    l = HG_CHUNK
    nc = q_ref.shape[0] // l
    st_ref[...] = jnp.zeros_like(st_ref)
    tri_ref[0] = _tri3(l, False)
    tri_ref[1] = _tri3(l, True)

    def rows_of(c):
        return pl.ds(pl.multiple_of(c * l, l), l)

    def load(c, rev, finalize):
        rows = rows_of(c)
        vals = (q_ref[rows, :], (fb_ref if rev else ff_ref)[rows, :], v_ref[rows, :])
        if finalize:
            vals += (h_ref[rows, :], gg_ref[rows, :])
        return vals

    def compute(vals, rev, finalize, states):
        d = int(rev)
        return _hgrn_chunk(vals, rev, finalize, states, lb_ref[d:d + 1, :], ng_ref[...], tri_ref[d])

    def store(c, out, finalize):
        (out_ref if finalize else h_ref)[rows_of(c), :] = out

    _run_scans(nc, load, compute, store, [(st_ref.at[0],), (st_ref.at[1],)])


def _hgrn(p16, p32, lb, norm_g, bsz, seq):
    n = p16.shape[0]
    blk16 = lambda grp: pl.BlockSpec(
        (seq, HG_DIM), lambda b, h, grp=grp: (b, COL16_HG // HG_DIM + grp * HG_HEADS + h))
    blk32 = lambda grp: pl.BlockSpec(
        (seq, HG_DIM), lambda b, h, grp=grp: (b, COL32_HGF // HG_DIM + grp * HG_HEADS + h))
    return pl.pallas_call(
        _hgrn_kernel,
        grid=(bsz, HG_HEADS),
        in_specs=[
            blk16(0), blk16(1), blk16(2), blk32(0), blk32(1),
            pl.BlockSpec((2, HG_DIM), lambda b, h: (0, h)),
            pl.BlockSpec((1, HG_DIM), lambda b, h: (0, h)),
        ],
        out_specs=pl.BlockSpec((seq, HG_DIM), lambda b, h: (b, h)),
        out_shape=jax.ShapeDtypeStruct((n, HG_HEADS * HG_DIM), BF16),
        scratch_shapes=[pltpu.VMEM((seq, HG_DIM), F32), pltpu.VMEM((2, HG_DIM, HG_DIM), F32),
                        pltpu.VMEM((2, HG_CHUNK, 3 * HG_CHUNK), BF16)],
        compiler_params=_params("arbitrary", "arbitrary"),
        name="hgrn2",
    )(p16, p16, p16, p32, p32, lb, norm_g)


def _merge_kernel(x_ref, ym_ref, yh_ref, gm_ref, gh_ref, wm_ref, wh_ref, wo_ref, o_ref):
    pm = _dot(ym_ref[...], wm_ref[...])
    ph = _dot(yh_ref[...], wh_ref[...])
    merged = _sigmoid(gm_ref[...].astype(F32)) * pm + _sigmoid(gh_ref[...].astype(F32)) * ph
    o_ref[...] = x_ref[...] + _dot(merged.astype(BF16), wo_ref[...])


def _merge(x, y_m, y_h, p16, w_m, w_h, w_o, tm):
    n, d = x.shape
    tm = min(tm, n)
    row = lambda cb: pl.BlockSpec((tm, d), lambda i, cb=cb: (i, cb))
    full = pl.BlockSpec((d, d), lambda i: (0, 0))
    return pl.pallas_call(
        _merge_kernel,
        grid=(n // tm,),
        in_specs=[row(0), row(0), row(0), row(COL16_GATE // d), row(COL16_GATE // d + 1), full, full, full],
        out_specs=row(0),
        out_shape=jax.ShapeDtypeStruct((n, d), F32),
        compiler_params=_params("arbitrary"),
        name="merge",
    )(x, y_m, y_h, p16, p16, w_m, w_h, w_o)


def _xattn_kernel(x_ref, g_ref, wq_ref, kv_ref, wo_ref, o_ref):
    x = x_ref[...]
    u = _rms(x, g_ref[...]).astype(BF16)
    q = _dot(u, wq_ref[...])
    heads = []
    for h in range(X_HEADS):
        lo = h * X_HEAD_DIM
        qh = q[:, lo:lo + X_HEAD_DIM].astype(BF16)
        kh = kv_ref[:, lo:lo + X_HEAD_DIM]
        vh = kv_ref[:, D_MODEL + lo:D_MODEL + lo + X_HEAD_DIM]
        s = _dot_nt(qh, kh) * (X_HEAD_DIM ** -0.5)
        p = jnp.exp(s - jnp.max(s, axis=-1, keepdims=True))
        p = p * (1.0 / jnp.sum(p, axis=-1, keepdims=True))
        heads.append(_dot(p.astype(BF16), vh).astype(BF16))
    att = jnp.concatenate(heads, axis=-1)
    o_ref[...] = x + _dot(att, wo_ref[...])


def _xattn(x, g, w_q, kv, w_o, seq, n_mem, tm):
    n, d = x.shape
    tm = min(tm, seq)
    per_seq = seq // tm
    full = pl.BlockSpec((d, d), lambda i: (0, 0))
    return pl.pallas_call(
        _xattn_kernel,
        grid=(n // tm,),
        in_specs=[
            pl.BlockSpec((tm, d), lambda i: (i, 0)),
            pl.BlockSpec((1, d), lambda i: (0, 0)),
            full,
            pl.BlockSpec((n_mem, 2 * d), lambda i: (i // per_seq, 0)),
            full,
        ],
        out_specs=pl.BlockSpec((tm, d), lambda i: (i, 0)),
        out_shape=jax.ShapeDtypeStruct((n, d), F32),
        compiler_params=_params("arbitrary"),
        name="xattn",
    )(x, g.reshape(1, d), w_q, kv, w_o)


def _gelu(x):
    return 0.5 * x * (1.0 + jnp.tanh(0.7978845608028654 * (x + 0.044715 * (x * x * x))))


def _ffn_kernel(x_ref, xp_ref, xn_ref, g_ref, wup_ref, cw_ref, cb_ref, wdn_ref, gf_ref, o_ref, *,
                per_seq, final_norm):
    i = pl.program_id(0)
    tm = x_ref.shape[0]
    x = x_ref[...]
    g = g_ref[...]
    first = (i % per_seq) == 0
    last = (i % per_seq) == per_seq - 1
    up_prev = jnp.where(first, 0.0, _rms(xp_ref[...], g))
    up_next = jnp.where(last, 0.0, _rms(xn_ref[...], g))
    u_ext = jnp.concatenate([up_prev, _rms(x, g), up_next], axis=0).astype(BF16)
    ext = tm + 2 * SUBLANES
    acc = x
    for c0 in range(0, D_FF, FF_CHUNK):
        a_ext = _dot(u_ext, wup_ref[:, c0:c0 + FF_CHUNK])
        gate = _dot(u_ext, wup_ref[:, D_FF + c0:D_FF + c0 + FF_CHUNK])[SUBLANES:SUBLANES + tm, :]
        a_prev = pltpu.roll(a_ext, 1, axis=0)[SUBLANES:SUBLANES + tm, :]
        a_next = pltpu.roll(a_ext, ext - 1, axis=0)[SUBLANES:SUBLANES + tm, :]
        a_cur = a_ext[SUBLANES:SUBLANES + tm, :]
        a = (cw_ref[0:1, c0:c0 + FF_CHUNK] * a_prev + cw_ref[1:2, c0:c0 + FF_CHUNK] * a_cur
             + cw_ref[2:3, c0:c0 + FF_CHUNK] * a_next + cb_ref[:, c0:c0 + FF_CHUNK])
        acc = acc + _dot((_gelu(a) * gate).astype(BF16), wdn_ref[c0:c0 + FF_CHUNK, :])
    o_ref[...] = _rms(acc, gf_ref[...]) if final_norm else acc


def _ffn(x, g, w_up, conv_w, conv_b, w_dn, g_final, final_norm, seq, tm):
    n, d = x.shape
    tm = min(tm, seq)
    per_seq = seq // tm
    tb = tm // SUBLANES
    nb = n // SUBLANES
    const = lambda shape: pl.BlockSpec(shape, lambda i: (0, 0))
    return pl.pallas_call(
        functools.partial(_ffn_kernel, per_seq=per_seq, final_norm=final_norm),
        grid=(n // tm,),
        in_specs=[
            pl.BlockSpec((tm, d), lambda i: (i, 0)),
            pl.BlockSpec((SUBLANES, d), lambda i: (jnp.maximum(i * tb - 1, 0), 0)),
            pl.BlockSpec((SUBLANES, d), lambda i: (jnp.minimum((i + 1) * tb, nb - 1), 0)),
            const((1, d)),
            const((d, 2 * D_FF)),
            const((3, D_FF)),
            const((1, D_FF)),
            const((D_FF, d)),
            const((1, d)),
        ],
        out_specs=pl.BlockSpec((tm, d), lambda i: (i, 0)),
        out_shape=jax.ShapeDtypeStruct((n, d), F32),
        compiler_params=_params("arbitrary"),
        name="ffn",
    )(x, x, x, g.reshape(1, d), w_up, conv_w, conv_b.reshape(1, D_FF), w_dn, g_final.reshape(1, d))


def kernel(x, mem, norm_mix_g, w_in, ml_conv_w, ml_conv_b, ml_gate_b, ml_norm_g, hg_lb, hg_norm_g,
           w_branch_m, w_branch_h, w_out, norm_x_g, norm_mem_g, w_q_x, w_kv_x, w_o_x, norm_ffn_g,
           w_up, ffn_conv_w, ffn_conv_b, w_down, norm_final_g):
    bsz, seq, d = x.shape
    n_mem = mem.shape[1]
    depth = w_in.shape[0]
    n = bsz * seq
    xf = x.reshape(n, d)
    memf = mem.reshape(bsz * n_mem, d)
    lb_all = jnp.cumsum(jax.nn.softmax(hg_lb.astype(F32), axis=1), axis=1)

    for l in range(depth):
        w = w_in[l]
        wg = w[:, 4096:4096 + 4 * ML_HEADS].reshape(d, 4, ML_HEADS).transpose(0, 2, 1)
        wg = jnp.pad(wg, ((0, 0), (0, 0), (0, LANES - 4))).reshape(d, ML_HEADS * LANES)
        gate_b = jnp.pad(ml_gate_b[l].T, ((0, 0), (0, LANES - 4))).reshape(1, ML_HEADS * LANES)
        o = 4096 + 4 * ML_HEADS
        hg_q, hg_ff, hg_fb, hg_i, hg_g, gates = (w[:, o:o + 1024], w[:, o + 1024:o + 2048],
                                                 w[:, o + 2048:o + 3072], w[:, o + 3072:o + 4096],
                                                 w[:, o + 4096:o + 5120], w[:, o + 5120:])
        w16 = jnp.concatenate([w[:, :4096], hg_q, hg_i, hg_g, gates], axis=1).astype(BF16)
        w32 = jnp.concatenate([hg_ff, hg_fb, wg], axis=1).astype(BF16)

        p16 = _norm_matmul(xf, norm_mix_g[l], w16, BF16, TM_PROJ, TN_PROJ)
        p32 = _norm_matmul(xf, norm_mix_g[l], w32, F32, TM_PROJ, TN_PROJ)
        y_m = _mlstm(p16, p32, ml_conv_w[l], ml_conv_b[l].reshape(1, -1), gate_b,
                     ml_norm_g[l].reshape(1, -1), bsz, seq)
        y_h = _hgrn(p16, p32, lb_all[:, l], hg_norm_g[l].reshape(1, -1), bsz, seq)
        xf = _merge(xf, y_m, y_h, p16, w_branch_m[l].astype(BF16), w_branch_h[l].astype(BF16),
                    w_out[l].astype(BF16), TM_MERGE)

        kv = _norm_matmul(memf, norm_mem_g[l], w_kv_x[l].astype(BF16), BF16, TM_PROJ, TN_PROJ)
        xf = _xattn(xf, norm_x_g[l], w_q_x[l].astype(BF16), kv, w_o_x[l].astype(BF16), seq, n_mem, TM_XATTN)

        xf = _ffn(xf, norm_ffn_g[l], w_up[l].astype(BF16), ffn_conv_w[l], ffn_conv_b[l],
                  w_down[l].astype(BF16), norm_final_g, l == depth - 1, seq, TM_FFN)
    return xf.reshape(bsz, seq, d)
```

```python
import functools

import jax
import jax.numpy as jnp
from jax import lax
from jax.experimental import pallas as pl
from jax.experimental.pallas import tpu as pltpu

F32 = jnp.float32
BF16 = jnp.bfloat16

EPS = 1e-6
D_MODEL = 1024
ML_HEADS = 4
ML_HEAD_DIM = 256
ML_CHUNK = 128
HG_HEADS = 8
HG_DIM = 128
HG_CHUNK = 128
X_HEADS = 4
X_HEAD_DIM = 256
D_FF = 2816
FF_CHUNK = 2816
SCAN_UNROLL = 4
TM_PROJ, TN_PROJ = 2048, 512
TM_MERGE = 1024
TM_XATTN = 1024
TM_FFN = 1024

V7X_VMEM_LIMIT_BYTES = 56 * 1024 * 1024
SUBLANES = 8
BF16_ROWS = 16
LANES = 128

COL16_ML = 0
COL16_HG = 4 * 1024
COL16_GATE = 7 * 1024
COLS16 = 9 * 1024
COL32_HGF = 0
COL32_MLG = 2 * 1024
COLS32 = COL32_MLG + ML_HEADS * LANES


def _params(*sem):
    return pltpu.CompilerParams(dimension_semantics=sem, vmem_limit_bytes=V7X_VMEM_LIMIT_BYTES)


def _sigmoid(x):
    return 0.5 * jnp.tanh(0.5 * x) + 0.5


def _log_sigmoid(x):
    return jnp.minimum(x, 0.0) - jnp.log(1.0 + jnp.exp(-jnp.abs(x)))


def _rms(x, g):
    return x * lax.rsqrt(jnp.mean(x * x, axis=-1, keepdims=True) + EPS) * g


def _dot(a, b):
    return jnp.dot(a, b, preferred_element_type=F32)


def _dot_nt(a, b):
    return lax.dot_general(a, b, (((1,), (1,)), ((), ())), preferred_element_type=F32)


def _dot_tn(a, b):
    return lax.dot_general(a, b, (((0,), (0,)), ((), ())), preferred_element_type=F32)


def _run_scans(nc, load, compute, store, state_refs):
    half = nc // 2

    def loop(lo, hi, finalize):
        def body(i, carry):
            steps = [(i * SCAN_UNROLL + j, rev) for j in range(SCAN_UNROLL) for rev in (False, True)]
            chunks = [(nc - 1 - s) if rev else s for s, rev in steps]
            loaded = [load(c, rev, finalize) for c, (_, rev) in zip(chunks, steps)]
            states = [[r[...] for r in refs] for refs in state_refs]
            gens = [compute(vals, rev, finalize, states) for vals, (_, rev) in zip(loaded, steps)]
            outs = [None] * len(gens)
            live = list(range(len(gens)))
            while live:
                for k in list(live):
                    try:
                        next(gens[k])
                    except StopIteration as done:
                        outs[k] = done.value
                        live.remove(k)
            for c, out in zip(chunks, outs):
                store(c, out, finalize)
            for refs, vals in zip(state_refs, states):
                for r, v in zip(refs, vals):
                    r[...] = v
            return carry

        lax.fori_loop(lo // SCAN_UNROLL, hi // SCAN_UNROLL, body, 0)

    loop(0, half, False)
    loop(half, nc, True)


def _norm_matmul_kernel(x_ref, g_ref, w_ref, o_ref, z_ref):
    @pl.when(pl.program_id(1) == 0)
    def _():
        z_ref[...] = _rms(x_ref[...], g_ref[...]).astype(BF16)

    o_ref[...] = _dot(z_ref[...], w_ref[...]).astype(o_ref.dtype)


def _norm_matmul(x, g, w, out_dtype, tm, tn):
    n, k = x.shape
    m = w.shape[1]
    tm = min(tm, n)
    tn = min(tn, m)
    return pl.pallas_call(
        _norm_matmul_kernel,
        grid=(n // tm, m // tn),
        in_specs=[
            pl.BlockSpec((tm, k), lambda i, j: (i, 0)),
            pl.BlockSpec((1, k), lambda i, j: (0, 0)),
            pl.BlockSpec((k, tn), lambda i, j: (0, j)),
        ],
        out_specs=pl.BlockSpec((tm, tn), lambda i, j: (i, j)),
        out_shape=jax.ShapeDtypeStruct((n, m), out_dtype),
        scratch_shapes=[pltpu.VMEM((tm, k), BF16)],
        compiler_params=_params("arbitrary", "arbitrary"),
        name="norm_matmul",
    )(x, g.reshape(1, k), w)


def _in_proj_kernel(x_ref, g_ref, w_ref, lb_ref, o16_ref, o32_ref, z_ref, *, tiles16, silu_tiles, gate_tiles):
    j = pl.program_id(1)

    @pl.when(j == 0)
    def _():
        z_ref[...] = _rms(x_ref[...], g_ref[...]).astype(BF16)

    is_silu = (j >= silu_tiles[0]) & (j < silu_tiles[1])
    is_gate = (j >= gate_tiles[0]) & (j < gate_tiles[1])

    @pl.when((j < tiles16) & jnp.logical_not(is_silu))
    def _():
        o16_ref[...] = _dot(z_ref[...], w_ref[...]).astype(BF16)

    @pl.when(is_silu)
    def _():
        acc = _dot(z_ref[...], w_ref[...])
        o16_ref[...] = (acc * _sigmoid(acc)).astype(BF16)

    @pl.when(is_gate)
    def _():
        lb = lb_ref[...]
        o32_ref[...] = lb + (1.0 - lb) * _sigmoid(_dot(z_ref[...], w_ref[...]))

    @pl.when((j >= tiles16) & jnp.logical_not(is_gate))
    def _():
        o32_ref[...] = _dot(z_ref[...], w_ref[...])


def _in_proj(x, g, w, lb_row, cols16, cols32, silu_cols, gate_cols, tm, tn):
    n, k = x.shape
    tm = min(tm, n)
    tiles16, tiles32 = cols16 // tn, cols32 // tn
    assert w.shape[1] == cols16 + cols32
    return pl.pallas_call(
        functools.partial(_in_proj_kernel, tiles16=tiles16, silu_tiles=tuple(c // tn for c in silu_cols),
                          gate_tiles=tuple(c // tn for c in gate_cols)),
        grid=(n // tm, tiles16 + tiles32),
        in_specs=[
            pl.BlockSpec((tm, k), lambda i, j: (i, 0)),
            pl.BlockSpec((1, k), lambda i, j: (0, 0)),
            pl.BlockSpec((k, tn), lambda i, j: (0, j)),
            pl.BlockSpec((1, tn), lambda i, j: (0, jnp.maximum(j - tiles16, 0))),
        ],
        out_specs=[
            pl.BlockSpec((tm, tn), lambda i, j: (i, jnp.minimum(j, tiles16 - 1))),
            pl.BlockSpec((tm, tn), lambda i, j: (i, jnp.maximum(j - tiles16, 0))),
        ],
        out_shape=[jax.ShapeDtypeStruct((n, cols16), BF16), jax.ShapeDtypeStruct((n, cols32), F32)],
        scratch_shapes=[pltpu.VMEM((tm, k), BF16)],
        compiler_params=_params("arbitrary", "arbitrary"),
        name="in_proj",
    )(x, g.reshape(1, k), w, lb_row)


def _load_with_halo(ref, c):
    l = ML_CHUNK
    r0 = pl.multiple_of(c * l, l)
    before = pl.multiple_of(jnp.maximum(r0 - BF16_ROWS, 0), BF16_ROWS)
    after = pl.multiple_of(jnp.minimum(r0 + l, ref.shape[0] - BF16_ROWS), BF16_ROWS)
    return ref[pl.ds(r0, l), :], ref[pl.ds(before, BF16_ROWS), :], ref[pl.ds(after, BF16_ROWS), :]


def _conv_silu(halo, cw, cb, has_prev, has_next):
    cur, before, after = (t.astype(F32) for t in halo)
    l = cur.shape[0]
    prev_row = jnp.where(has_prev, before[BF16_ROWS - 1:BF16_ROWS, :], 0.0)
    next_row = jnp.where(has_next, after[0:1, :], 0.0)
    row = lax.broadcasted_iota(jnp.int32, cur.shape, 0)
    x_prev = jnp.where(row == 0, prev_row, pltpu.roll(cur, 1, axis=0))
    x_next = jnp.where(row == l - 1, next_row, pltpu.roll(cur, l - 1, axis=0))
    a = cw[0:1, :] * x_prev + cw[1:2, :] * cur + cw[2:3, :] * x_next + cb
    return a * _sigmoid(a)


def _mlstm_chunk(vals, rev, finalize, states, params, tri3t):
    l = ML_CHUNK
    d = int(rev)
    cwq, cwk, cbq, cbk, gate_b, norm_g = params
    if finalize:
        kb, q_t, k_t, vb, g_rows = vals[:5]
    else:
        q_halo, k_halo, vb, g, has_prev, has_next = vals
        qc = _conv_silu(q_halo, cwq, cbq, has_prev, has_next)
        kc = _conv_silu(k_halo, cwk, cbk, has_prev, has_next) * (ML_HEAD_DIM ** -0.5)
        kb, q_t, k_t = kc.astype(BF16), qc.T.astype(BF16), kc.T.astype(BF16)
        g_rows = (g + gate_b).T[0:SUBLANES, :]
        yield
    ic, fc = (2, 3) if rev else (0, 1)

    i_row = g_rows[ic:ic + 1, :]
    ls16 = jnp.concatenate([_log_sigmoid(g_rows), jnp.zeros_like(g_rows)], axis=0)
    bcum16 = _dot(_split3(ls16, axis=1), tri3t)
    kq = _dot(kb, q_t)
    yield
    bcum_row = bcum16[fc:fc + 1, :]
    g_end = bcum_row[:, 0:1] if rev else bcum_row[:, l - 1:l]
    a_row = i_row - bcum_row
    rid = lax.broadcasted_iota(jnp.int32, (2 * SUBLANES, l), 0)
    a3, b3 = _pieces3(a_row), _pieces3(bcum_row)
    lhs = jnp.where(rid == 0, a3[0], jnp.where(rid == 1, a3[1], jnp.where(rid == 2, a3[2],
                    jnp.where(rid < 6, 1.0, 0.0))))
    rhs = jnp.where(rid == 3, b3[0], jnp.where(rid == 4, b3[1], jnp.where(rid == 5, b3[2],
                    jnp.where(rid < 3, 1.0, 0.0))))
    s_idx = lax.broadcasted_iota(jnp.int32, (l, l), 0)
    t_idx = lax.broadcasted_iota(jnp.int32, (l, l), 1)
    pair = (s_idx >= t_idx) if rev else (s_idx <= t_idx)
    outer = _dot_tn(lhs.astype(BF16), rhs.astype(BF16))
    yield
    m_st = states[d][2]
    m_prev = m_st[0:1, 0:1]
    log_d = jnp.where(pair, outer, -jnp.inf)
    m_inter = bcum_row + m_prev
    m_t = jnp.maximum(m_inter, jnp.max(log_d, axis=0, keepdims=True))
    dmat = jnp.exp(log_d - m_t)
    w_inter = jnp.exp(m_inter - m_t)
    log_w = g_end + a_row
    m_new = jnp.maximum(g_end + m_prev, jnp.max(log_w, axis=1, keepdims=True))
    wgt = jnp.exp(log_w - m_new)
    decay = jnp.exp(g_end + m_prev - m_new)
    states[d][2] = jnp.broadcast_to(m_new, m_st.shape)
    yield

    c_st, n_st = states[d][0], states[d][1]
    c_b = c_st.astype(BF16)
    q_n = _dot(n_st.astype(BF16), q_t)[0:1, :]
    wgt16 = jnp.where(rid == 0, wgt, 0.0).astype(BF16)
    states[d][0] = decay * c_st + _dot((k_t.astype(F32) * wgt).astype(BF16), vb)
    states[d][1] = decay * n_st + _dot(wgt16, kb)
    yield

    sc = kq * dmat
    den = jnp.sum(sc, axis=0, keepdims=True) + w_inter * q_n
    scale = 1.0 / jnp.maximum(jnp.abs(den), jnp.exp(-m_t))
    lhs_t = jnp.concatenate([(sc * scale).astype(BF16),
                             (q_t.astype(F32) * (w_inter * scale)).astype(BF16)], axis=0)
    hc = _dot_tn(lhs_t, jnp.concatenate([vb, c_b], axis=0))
    if not finalize:
        return hc, kb, q_t, k_t, g_rows
    yield
    prev, og = vals[5:]
    hs = prev + hc
    y = hs * lax.rsqrt(jnp.mean(hs * hs, axis=-1, keepdims=True) + EPS) * norm_g
    return ((y * _sigmoid(og.astype(F32))).astype(BF16),)


def _mlstm_kernel(q_ref, k_ref, v_ref, og_ref, g_ref, cwq_ref, cwk_ref, cbq_ref, cbk_ref, gb_ref, ng_ref,
                  out_ref, h_ref, ks_ref, qt_ref, kt_ref, gt_ref, tri_ref, c_ref, n_ref, m_ref):
    l = ML_CHUNK
    nc = q_ref.shape[0] // l

    def rows_of(c):
        return pl.ds(pl.multiple_of(c * l, l), l)

    c_ref[...] = jnp.zeros_like(c_ref)
    n_ref[...] = jnp.zeros_like(n_ref)
    m_ref[...] = jnp.zeros_like(m_ref)
    tri_ref[0] = _tri3t(l, False)
    tri_ref[1] = _tri3t(l, True)

    def load(c, rev, finalize):
        rows = rows_of(c)
        if finalize:
            return (ks_ref[rows, :], qt_ref[:, rows], kt_ref[:, rows], v_ref[rows, :], gt_ref[:, rows],
                    h_ref[rows, :], og_ref[rows, :])
        return (_load_with_halo(q_ref, c), _load_with_halo(k_ref, c), v_ref[rows, :], g_ref[rows, :],
                c > 0, c < nc - 1)

    def compute(vals, rev, finalize, states):
        params = (cwq_ref[...], cwk_ref[...], cbq_ref[...], cbk_ref[...], gb_ref[...], ng_ref[...])
        return _mlstm_chunk(vals, rev, finalize, states, params, tri_ref[int(rev)])

    def store(c, out, finalize):
        rows = rows_of(c)
        if finalize:
            out_ref[rows, :] = out[0]
        else:
            h_ref[rows, :], ks_ref[rows, :], qt_ref[:, rows], kt_ref[:, rows], gt_ref[:, rows] = out

    _run_scans(nc, load, compute, store, [(c_ref.at[d], n_ref.at[d], m_ref.at[d]) for d in (0, 1)])


def _mlstm(p16, p32, conv_w, conv_b, gate_b, norm_g, bsz, seq):
    n = p16.shape[0]
    hd = ML_HEAD_DIM
    blk = lambda off: pl.BlockSpec((seq, hd), lambda b, h, off=off: (b, COL16_ML // hd + off + h))
    wblk = lambda rows, off: pl.BlockSpec((rows, hd), lambda b, h, off=off: (0, off + h))
    return pl.pallas_call(
        _mlstm_kernel,
        grid=(bsz, ML_HEADS),
        in_specs=[
            blk(0), blk(ML_HEADS), blk(2 * ML_HEADS), blk(3 * ML_HEADS),
            pl.BlockSpec((seq, LANES), lambda b, h: (b, COL32_MLG // LANES + h)),
            wblk(3, 0), wblk(3, ML_HEADS), wblk(1, 0), wblk(1, ML_HEADS),
            pl.BlockSpec((1, LANES), lambda b, h: (0, h)),
            wblk(1, 0),
        ],
        out_specs=pl.BlockSpec((seq, hd), lambda b, h: (b, h)),
        out_shape=jax.ShapeDtypeStruct((n, ML_HEADS * hd), BF16),
        scratch_shapes=[pltpu.VMEM((seq, hd), F32),
                        pltpu.VMEM((seq, hd), BF16), pltpu.VMEM((hd, seq), BF16), pltpu.VMEM((hd, seq), BF16),
                        pltpu.VMEM((SUBLANES, seq), F32), pltpu.VMEM((2, 3 * ML_CHUNK, ML_CHUNK), BF16),
                        pltpu.VMEM((2, hd, hd), F32), pltpu.VMEM((2, 2 * SUBLANES, hd), F32),
                        pltpu.VMEM((2, 1, LANES), F32)],
        compiler_params=_params("arbitrary", "arbitrary"),
        name="mlstm",
    )(p16, p16, p16, p16, p32, conv_w, conv_w, conv_b, conv_b, gate_b, norm_g)


def _pieces3(x):
    hi = x.astype(BF16).astype(F32)
    r1 = x - hi
    mid = r1.astype(BF16).astype(F32)
    return hi, mid, r1 - mid


def _split3(x, axis=0):
    return jnp.concatenate([p.astype(BF16) for p in _pieces3(x)], axis=axis)


def _tri3(l, rev):
    t = lax.broadcasted_iota(jnp.int32, (l, 3 * l), 0)
    u = lax.broadcasted_iota(jnp.int32, (l, 3 * l), 1) & (l - 1)
    inside = (u >= t) if rev else (u <= t)
    return jnp.where(inside, 1.0, 0.0).astype(BF16)


def _tri3t(l, rev):
    u = lax.broadcasted_iota(jnp.int32, (3 * l, l), 0) & (l - 1)
    t = lax.broadcasted_iota(jnp.int32, (3 * l, l), 1)
    inside = (u >= t) if rev else (u <= t)
    return jnp.where(inside, 1.0, 0.0).astype(BF16)


def _cumsum_rows(x, tri3):
    return _dot(tri3, _split3(x))


def _level_factors(b, f, q, kk, w, rev, row):
    l = b.shape[0]
    late = (row & w) != 0
    q_side = (~late) if rev else late
    if w == 1:
        return jnp.where(q_side, q * f, kk)
    if w == 2:
        p = row & 3
        f_prev = pltpu.roll(f, 1, axis=0)
        f_next = pltpu.roll(f, l - 1, axis=0)
        if rev:
            e = jnp.where(p == 0, f * f_next, jnp.where(p == 1, f, jnp.where(p == 3, f_prev, 1.0)))
        else:
            e = jnp.where(p == 3, f * f_prev, jnp.where(p == 2, f, jnp.where(p == 0, f_next, 1.0)))
        return jnp.where(q_side, q, kk) * e
    if w == 4:
        pieces = []
        for start in range(0, l, 2 * w):
            r = start + w if rev else start + w - 1
            pieces.append(jnp.broadcast_to(b[r:r + 1, :], (2 * w, b.shape[1])))
        return jnp.where(q_side, q, kk) * jnp.exp2(-jnp.abs(b - jnp.concatenate(pieces, axis=0)))
    base, expo = [], []
    for start in range(0, l, 2 * w):
        lo, hi = slice(start, start + w), slice(start + w, start + 2 * w)
        if rev:
            r = b[start + w:start + w + 1, :]
            base += [q[lo], kk[hi]]
            expo += [b[lo] - r, r - b[hi]]
        else:
            r = b[start + w - 1:start + w, :]
            base += [kk[lo], q[hi]]
            expo += [r - b[lo], b[hi] - r]
    return jnp.concatenate(base, axis=0) * jnp.exp2(jnp.concatenate(expo, axis=0))


def _hgrn_chunk(vals, rev, finalize, states, norm_g, tri3):
    l = HG_CHUNK
    d = int(rev)
    qb, f, vb = vals[:3]
    q = qb.astype(F32)
    kk = 1.0 - f
    b = _cumsum_rows(jnp.log2(f), tri3)
    yield

    st = states[d][0]
    o_inter = _dot_nt((q * jnp.exp2(b)).astype(BF16), st.astype(BF16))
    b_end = b[0:1, :] if rev else b[l - 1:l, :]
    kd = (kk * jnp.exp2(b_end - b)).astype(BF16)
    states[d][0] = jnp.exp2(b_end) * st + _dot_tn(vb, kd)


    row = lax.broadcasted_iota(jnp.int32, (l, HG_DIM), 0)
    t_idx = lax.broadcasted_iota(jnp.int32, (l, l), 0)
    s_idx = lax.broadcasted_iota(jnp.int32, (l, l), 1)
    diff = t_idx ^ s_idx
    att = None
    w = l // 2
    while w >= 1:
        mb = _level_factors(b, f, q, kk, w, rev, row).astype(BF16)
        a = _dot_nt(mb, mb)
        att = a if att is None else jnp.where(diff < 2 * w, a, att)
        w //= 2
    yield

    keep = (s_idx > t_idx) if rev else (s_idx < t_idx)
    att = jnp.where(diff == 0, jnp.sum(q * kk, axis=1, keepdims=True), jnp.where(keep, att, 0.0))
    o = _dot(att.astype(BF16), vb) + o_inter
    if finalize:
        yield
        prev, gg = vals[3:]
        gg = gg.astype(F32)
        hs = prev + o
        y = hs * lax.rsqrt(jnp.mean(hs * hs, axis=-1, keepdims=True) + EPS) * norm_g
        o = (y * (gg * _sigmoid(gg))).astype(BF16)
    return o


def _hgrn_kernel(q_ref, v_ref, gg_ref, ff_ref, fb_ref, ng_ref, out_ref, h_ref, st_ref, tri_ref):
    l = HG_CHUNK
    nc = q_ref.shape[0] // l
    st_ref[...] = jnp.zeros_like(st_ref)
    tri_ref[0] = _tri3(l, False)
    tri_ref[1] = _tri3(l, True)

    def rows_of(c):
        return pl.ds(pl.multiple_of(c * l, l), l)

    def load(c, rev, finalize):
        rows = rows_of(c)
        vals = (q_ref[rows, :], (fb_ref if rev else ff_ref)[rows, :], v_ref[rows, :])
        if finalize:
            vals += (h_ref[rows, :], gg_ref[rows, :])
        return vals

    def compute(vals, rev, finalize, states):
        return _hgrn_chunk(vals, rev, finalize, states, ng_ref[...], tri_ref[int(rev)])

    def store(c, out, finalize):
        (out_ref if finalize else h_ref)[rows_of(c), :] = out

    _run_scans(nc, load, compute, store, [(st_ref.at[0],), (st_ref.at[1],)])


def _hgrn(p16, p32, norm_g, bsz, seq):
    n = p16.shape[0]
    blk16 = lambda grp: pl.BlockSpec(
        (seq, HG_DIM), lambda b, h, grp=grp: (b, COL16_HG // HG_DIM + grp * HG_HEADS + h))
    blk32 = lambda grp: pl.BlockSpec(
        (seq, HG_DIM), lambda b, h, grp=grp: (b, COL32_HGF // HG_DIM + grp * HG_HEADS + h))
    return pl.pallas_call(
        _hgrn_kernel,
        grid=(bsz, HG_HEADS),
        in_specs=[
            blk16(0), blk16(1), blk16(2), blk32(0), blk32(1),
            pl.BlockSpec((1, HG_DIM), lambda b, h: (0, h)),
        ],
        out_specs=pl.BlockSpec((seq, HG_DIM), lambda b, h: (b, h)),
        out_shape=jax.ShapeDtypeStruct((n, HG_HEADS * HG_DIM), BF16),
        scratch_shapes=[pltpu.VMEM((seq, HG_DIM), F32), pltpu.VMEM((2, HG_DIM, HG_DIM), F32),
                        pltpu.VMEM((2, HG_CHUNK, 3 * HG_CHUNK), BF16)],
        compiler_params=_params("arbitrary", "arbitrary"),
        name="hgrn2",
    )(p16, p16, p16, p32, p32, norm_g)


def _merge_kernel(x_ref, ym_ref, yh_ref, gm_ref, gh_ref, wm_ref, wh_ref, wo_ref, o_ref):
    pm = _dot(ym_ref[...], wm_ref[...])
    ph = _dot(yh_ref[...], wh_ref[...])
    merged = _sigmoid(gm_ref[...].astype(F32)) * pm + _sigmoid(gh_ref[...].astype(F32)) * ph
    o_ref[...] = x_ref[...] + _dot(merged.astype(BF16), wo_ref[...])


def _merge(x, y_m, y_h, p16, w_m, w_h, w_o, tm):
    n, d = x.shape
    tm = min(tm, n)
    row = lambda cb: pl.BlockSpec((tm, d), lambda i, cb=cb: (i, cb))
    full = pl.BlockSpec((d, d), lambda i: (0, 0))
    return pl.pallas_call(
        _merge_kernel,
        grid=(n // tm,),
        in_specs=[row(0), row(0), row(0), row(COL16_GATE // d), row(COL16_GATE // d + 1), full, full, full],
        out_specs=row(0),
        out_shape=jax.ShapeDtypeStruct((n, d), F32),
        compiler_params=_params("arbitrary"),
        name="merge",
    )(x, y_m, y_h, p16, p16, w_m, w_h, w_o)


def _xattn_kernel(x_ref, g_ref, wq_ref, kv_ref, wo_ref, o_ref):
    x = x_ref[...]
    u = _rms(x, g_ref[...]).astype(BF16)
    q = _dot(u, wq_ref[...])
    heads = []
    for h in range(X_HEADS):
        lo = h * X_HEAD_DIM
        qh = q[:, lo:lo + X_HEAD_DIM].astype(BF16)
        kh = kv_ref[:, lo:lo + X_HEAD_DIM]
        vh = kv_ref[:, D_MODEL + lo:D_MODEL + lo + X_HEAD_DIM]
        s = _dot_nt(qh, kh) * (X_HEAD_DIM ** -0.5)
        p = jnp.exp(s - jnp.max(s, axis=-1, keepdims=True))
        p = p * (1.0 / jnp.sum(p, axis=-1, keepdims=True))
        heads.append(_dot(p.astype(BF16), vh).astype(BF16))
    att = jnp.concatenate(heads, axis=-1)
    o_ref[...] = x + _dot(att, wo_ref[...])


def _xattn(x, g, w_q, kv, w_o, seq, n_mem, tm):
    n, d = x.shape
    tm = min(tm, seq)
    per_seq = seq // tm
    full = pl.BlockSpec((d, d), lambda i: (0, 0))
    return pl.pallas_call(
        _xattn_kernel,
        grid=(n // tm,),
        in_specs=[
            pl.BlockSpec((tm, d), lambda i: (i, 0)),
            pl.BlockSpec((1, d), lambda i: (0, 0)),
            full,
            pl.BlockSpec((n_mem, 2 * d), lambda i: (i // per_seq, 0)),
            full,
        ],
        out_specs=pl.BlockSpec((tm, d), lambda i: (i, 0)),
        out_shape=jax.ShapeDtypeStruct((n, d), F32),
        compiler_params=_params("arbitrary"),
        name="xattn",
    )(x, g.reshape(1, d), w_q, kv, w_o)


def _gelu(x):
    return 0.5 * x * (1.0 + jnp.tanh(0.7978845608028654 * (x + 0.044715 * (x * x * x))))


def _ffn_kernel(x_ref, xp_ref, xn_ref, g_ref, wup_ref, cw_ref, cb_ref, wdn_ref, gf_ref, o_ref, *,
                per_seq, final_norm):
    i = pl.program_id(0)
    tm = x_ref.shape[0]
    x = x_ref[...]
    g = g_ref[...]
    first = (i % per_seq) == 0
    last = (i % per_seq) == per_seq - 1
    up_prev = jnp.where(first, 0.0, _rms(xp_ref[...], g))
    up_next = jnp.where(last, 0.0, _rms(xn_ref[...], g))
    u_ext = jnp.concatenate([up_prev, _rms(x, g), up_next], axis=0).astype(BF16)
    ext = tm + 2 * SUBLANES
    acc = x
    for c0 in range(0, D_FF, FF_CHUNK):
        a_ext = _dot(u_ext, wup_ref[:, c0:c0 + FF_CHUNK])
        gate = _dot(u_ext, wup_ref[:, D_FF + c0:D_FF + c0 + FF_CHUNK])[SUBLANES:SUBLANES + tm, :]
        a_prev = pltpu.roll(a_ext, 1, axis=0)[SUBLANES:SUBLANES + tm, :]
        a_next = pltpu.roll(a_ext, ext - 1, axis=0)[SUBLANES:SUBLANES + tm, :]
        a_cur = a_ext[SUBLANES:SUBLANES + tm, :]
        a = (cw_ref[0:1, c0:c0 + FF_CHUNK] * a_prev + cw_ref[1:2, c0:c0 + FF_CHUNK] * a_cur
             + cw_ref[2:3, c0:c0 + FF_CHUNK] * a_next + cb_ref[:, c0:c0 + FF_CHUNK])
        acc = acc + _dot((_gelu(a) * gate).astype(BF16), wdn_ref[c0:c0 + FF_CHUNK, :])
    o_ref[...] = _rms(acc, gf_ref[...]) if final_norm else acc


def _ffn(x, g, w_up, conv_w, conv_b, w_dn, g_final, final_norm, seq, tm):
    n, d = x.shape
    tm = min(tm, seq)
    per_seq = seq // tm
    tb = tm // SUBLANES
    nb = n // SUBLANES
    const = lambda shape: pl.BlockSpec(shape, lambda i: (0, 0))
    return pl.pallas_call(
        functools.partial(_ffn_kernel, per_seq=per_seq, final_norm=final_norm),
        grid=(n // tm,),
        in_specs=[
            pl.BlockSpec((tm, d), lambda i: (i, 0)),
            pl.BlockSpec((SUBLANES, d), lambda i: (jnp.maximum(i * tb - 1, 0), 0)),
            pl.BlockSpec((SUBLANES, d), lambda i: (jnp.minimum((i + 1) * tb, nb - 1), 0)),
            const((1, d)),
            const((d, 2 * D_FF)),
            const((3, D_FF)),
            const((1, D_FF)),
            const((D_FF, d)),
            const((1, d)),
        ],
        out_specs=pl.BlockSpec((tm, d), lambda i: (i, 0)),
        out_shape=jax.ShapeDtypeStruct((n, d), F32),
        compiler_params=_params("arbitrary"),
        name="ffn",
    )(x, x, x, g.reshape(1, d), w_up, conv_w, conv_b.reshape(1, D_FF), w_dn, g_final.reshape(1, d))


def kernel(x, mem, norm_mix_g, w_in, ml_conv_w, ml_conv_b, ml_gate_b, ml_norm_g, hg_lb, hg_norm_g,
           w_branch_m, w_branch_h, w_out, norm_x_g, norm_mem_g, w_q_x, w_kv_x, w_o_x, norm_ffn_g,
           w_up, ffn_conv_w, ffn_conv_b, w_down, norm_final_g):
    bsz, seq, d = x.shape
    n_mem = mem.shape[1]
    depth = w_in.shape[0]
    n = bsz * seq
    xf = x.reshape(n, d)
    memf = mem.reshape(bsz * n_mem, d)
    lb_all = jnp.cumsum(jax.nn.softmax(hg_lb.astype(F32), axis=1), axis=1)

    for l in range(depth):
        w = w_in[l].astype(BF16)
        wg = w[:, 4096:4096 + 4 * ML_HEADS].reshape(d, 4, ML_HEADS).transpose(0, 2, 1)
        wg = jnp.pad(wg, ((0, 0), (0, 0), (0, LANES - 4))).reshape(d, ML_HEADS * LANES)
        gate_b = jnp.pad(ml_gate_b[l].T, ((0, 0), (0, LANES - 4))).reshape(1, ML_HEADS * LANES)
        o = 4096 + 4 * ML_HEADS
        hg_q, hg_ff, hg_fb, hg_i, hg_g, gates = (w[:, o:o + 1024], w[:, o + 1024:o + 2048],
                                                 w[:, o + 2048:o + 3072], w[:, o + 3072:o + 4096],
                                                 w[:, o + 4096:o + 5120], w[:, o + 5120:])
        w_proj = jnp.concatenate([w[:, :4096], hg_q, hg_i, hg_g, gates, hg_ff, hg_fb, wg], axis=1)
        lb_row = jnp.pad(lb_all[:, l].reshape(1, 2 * D_MODEL), ((0, 0), (0, ML_HEADS * LANES)))

        p16, p32 = _in_proj(xf, norm_mix_g[l], w_proj, lb_row, COLS16, COLS32,
                            (COL16_HG, COL16_HG + 1024), (COLS16 + COL32_HGF, COLS16 + COL32_MLG),
                            TM_PROJ, TN_PROJ)
        y_m = _mlstm(p16, p32, ml_conv_w[l], ml_conv_b[l].reshape(1, -1), gate_b,
                     ml_norm_g[l].reshape(1, -1), bsz, seq)
        y_h = _hgrn(p16, p32, hg_norm_g[l].reshape(1, -1), bsz, seq)
        xf = _merge(xf, y_m, y_h, p16, w_branch_m[l].astype(BF16), w_branch_h[l].astype(BF16),
                    w_out[l].astype(BF16), TM_MERGE)

        kv = _norm_matmul(memf, norm_mem_g[l], w_kv_x[l].astype(BF16), BF16, TM_PROJ, TN_PROJ)
        xf = _xattn(xf, norm_x_g[l], w_q_x[l].astype(BF16), kv, w_o_x[l].astype(BF16), seq, n_mem, TM_XATTN)

        xf = _ffn(xf, norm_ffn_g[l], w_up[l].astype(BF16), ffn_conv_w[l], ffn_conv_b[l],
                  w_down[l].astype(BF16), norm_final_g, l == depth - 1, seq, TM_FFN)
    return xf.reshape(bsz, seq, d)
```

```python
import functools

import jax
import jax.numpy as jnp
from jax import lax
from jax.experimental import pallas as pl
from jax.experimental.pallas import tpu as pltpu

F32 = jnp.float32
BF16 = jnp.bfloat16

EPS = 1e-6
D_MODEL = 1024
ML_HEADS = 4
ML_HEAD_DIM = 256
ML_CHUNK = 128
HG_HEADS = 8
HG_DIM = 128
HG_CHUNK = 128
X_HEADS = 4
X_HEAD_DIM = 256
D_FF = 2816
FF_CHUNK = 2816
ML_UNROLL = 4
HG_UNROLL = 8
TM_PROJ, TN_PROJ = 2048, 512
TM_INPROJ, TN_INPROJ = 512, 512
TM_MERGE = 1024
TM_XATTN = 1024
TM_FFN = 1024

V7X_VMEM_LIMIT_BYTES = 56 * 1024 * 1024
INPROJ_VMEM_LIMIT_BYTES = 58 * 1024 * 1024
SUBLANES = 8
BF16_ROWS = 16
LANES = 128

COL16_ML = 0
COL16_HG = 4 * 1024
COL16_GATE = 7 * 1024
COLS16 = 9 * 1024
COL32_HGF = 0
COL32_MLG = 2 * 1024
COLS32 = COL32_MLG + ML_HEADS * LANES


def _params(*sem, vmem_limit=V7X_VMEM_LIMIT_BYTES):
    return pltpu.CompilerParams(dimension_semantics=sem, vmem_limit_bytes=vmem_limit)


def _sigmoid(x):
    return 0.5 * jnp.tanh(0.5 * x) + 0.5


def _log_sigmoid(x):
    return jnp.minimum(x, 0.0) - jnp.log(1.0 + jnp.exp(-jnp.abs(x)))


def _rms(x, g):
    return x * lax.rsqrt(jnp.mean(x * x, axis=-1, keepdims=True) + EPS) * g


def _dot(a, b):
    return jnp.dot(a, b, preferred_element_type=F32)


def _dot_nt(a, b):
    return lax.dot_general(a, b, (((1,), (1,)), ((), ())), preferred_element_type=F32)


def _dot_tn(a, b):
    return lax.dot_general(a, b, (((0,), (0,)), ((), ())), preferred_element_type=F32)


def _run_scans(nc, unroll, load, compute, store, state_refs):
    half = nc // 2

    def loop(lo, hi, finalize):
        def body(i, carry):
            steps = [(i * unroll + j, rev) for j in range(unroll) for rev in (False, True)]
            chunks = [(nc - 1 - s) if rev else s for s, rev in steps]
            loaded = [load(c, rev, finalize) for c, (_, rev) in zip(chunks, steps)]
            states = [[r[...] for r in refs] for refs in state_refs]
            gens = [compute(vals, rev, finalize, states) for vals, (_, rev) in zip(loaded, steps)]
            outs = [None] * len(gens)
            live = list(range(len(gens)))
            while live:
                for k in list(live):
                    try:
                        next(gens[k])
                    except StopIteration as done:
                        outs[k] = done.value
                        live.remove(k)
            for c, out in zip(chunks, outs):
                store(c, out, finalize)
            for refs, vals in zip(state_refs, states):
                for r, v in zip(refs, vals):
                    r[...] = v
            return carry

        lax.fori_loop(lo // unroll, hi // unroll, body, 0)

    loop(0, half, False)
    loop(half, nc, True)


def _norm_matmul_kernel(x_ref, g_ref, w_ref, o_ref, z_ref):
    @pl.when(pl.program_id(1) == 0)
    def _():
        z_ref[...] = _rms(x_ref[...], g_ref[...]).astype(BF16)

    o_ref[...] = _dot(z_ref[...], w_ref[...]).astype(o_ref.dtype)


def _norm_matmul(x, g, w, out_dtype, tm, tn):
    n, k = x.shape
    m = w.shape[1]
    tm = min(tm, n)
    tn = min(tn, m)
    return pl.pallas_call(
        _norm_matmul_kernel,
        grid=(n // tm, m // tn),
        in_specs=[
            pl.BlockSpec((tm, k), lambda i, j: (i, 0)),
            pl.BlockSpec((1, k), lambda i, j: (0, 0)),
            pl.BlockSpec((k, tn), lambda i, j: (0, j)),
        ],
        out_specs=pl.BlockSpec((tm, tn), lambda i, j: (i, j)),
        out_shape=jax.ShapeDtypeStruct((n, m), out_dtype),
        scratch_shapes=[pltpu.VMEM((tm, k), BF16)],
        compiler_params=_params("arbitrary", "arbitrary"),
        name="norm_matmul",
    )(x, g.reshape(1, k), w)


def _in_proj_kernel(x_ref, g_ref, w_ref, lb_ref, o16_ref, o32_ref, *, tn, silu_cols, gate_cols):
    z = _rms(x_ref[...], g_ref[...]).astype(BF16)
    cols16 = o16_ref.shape[1]
    for c0 in range(0, w_ref.shape[1], tn):
        acc = _dot(z, w_ref[:, c0:c0 + tn])
        if c0 < cols16:
            if silu_cols[0] <= c0 < silu_cols[1]:
                acc = acc * _sigmoid(acc)
            o16_ref[:, c0:c0 + tn] = acc.astype(BF16)
        else:
            c32 = c0 - cols16
            if gate_cols[0] <= c0 < gate_cols[1]:
                lb = lb_ref[:, c32:c32 + tn]
                acc = lb + (1.0 - lb) * _sigmoid(acc)
            o32_ref[:, c32:c32 + tn] = acc


def _in_proj(x, g, w, lb_row, cols16, cols32, silu_cols, gate_cols, tm, tn):
    n, k = x.shape
    tm = min(tm, n)
    assert w.shape[1] == cols16 + cols32
    const = lambda shape: pl.BlockSpec(shape, lambda i: (0, 0))
    return pl.pallas_call(
        functools.partial(_in_proj_kernel, tn=tn, silu_cols=silu_cols, gate_cols=gate_cols),
        grid=(n // tm,),
        in_specs=[pl.BlockSpec((tm, k), lambda i: (i, 0)), const((1, k)), const(w.shape), const(lb_row.shape)],
        out_specs=[pl.BlockSpec((tm, cols16), lambda i: (i, 0)), pl.BlockSpec((tm, cols32), lambda i: (i, 0))],
        out_shape=[jax.ShapeDtypeStruct((n, cols16), BF16), jax.ShapeDtypeStruct((n, cols32), F32)],
        compiler_params=_params("arbitrary", vmem_limit=INPROJ_VMEM_LIMIT_BYTES),
        name="in_proj",
    )(x, g.reshape(1, k), w, lb_row)


def _load_with_halo(ref, c):
    l = ML_CHUNK
    r0 = pl.multiple_of(c * l, l)
    before = pl.multiple_of(jnp.maximum(r0 - BF16_ROWS, 0), BF16_ROWS)
    after = pl.multiple_of(jnp.minimum(r0 + l, ref.shape[0] - BF16_ROWS), BF16_ROWS)
    return ref[pl.ds(r0, l), :], ref[pl.ds(before, BF16_ROWS), :], ref[pl.ds(after, BF16_ROWS), :]


def _conv_silu(halo, cw, cb, has_prev, has_next):
    cur, before, after = (t.astype(F32) for t in halo)
    l = cur.shape[0]
    prev_row = jnp.where(has_prev, before[BF16_ROWS - 1:BF16_ROWS, :], 0.0)
    next_row = jnp.where(has_next, after[0:1, :], 0.0)
    row = lax.broadcasted_iota(jnp.int32, cur.shape, 0)
    x_prev = jnp.where(row == 0, prev_row, pltpu.roll(cur, 1, axis=0))
    x_next = jnp.where(row == l - 1, next_row, pltpu.roll(cur, l - 1, axis=0))
    a = cw[0:1, :] * x_prev + cw[1:2, :] * cur + cw[2:3, :] * x_next + cb
    return a * _sigmoid(a)


def _mlstm_chunk(vals, rev, finalize, states, params, tri3t):
    l = ML_CHUNK
    d = int(rev)
    cwq, cwk, cbq, cbk, gate_b, norm_g = params
    if finalize:
        kb, q_t, k_t, vb, g_rows = vals[:5]
    else:
        q_halo, k_halo, vb, g, has_prev, has_next = vals
        qc = _conv_silu(q_halo, cwq, cbq, has_prev, has_next)
        kc = _conv_silu(k_halo, cwk, cbk, has_prev, has_next) * (ML_HEAD_DIM ** -0.5)
        kb, q_t, k_t = kc.astype(BF16), qc.T.astype(BF16), kc.T.astype(BF16)
        g_rows = (g + gate_b).T[0:SUBLANES, :]
        yield
    ic, fc = (2, 3) if rev else (0, 1)

    i_row = g_rows[ic:ic + 1, :]
    ls16 = jnp.concatenate([_log_sigmoid(g_rows), jnp.zeros_like(g_rows)], axis=0)
    bcum16 = _dot(_split3(ls16, axis=1), tri3t)
    kq = _dot(kb, q_t)
    yield
    bcum_row = bcum16[fc:fc + 1, :]
    g_end = bcum_row[:, 0:1] if rev else bcum_row[:, l - 1:l]
    a_row = i_row - bcum_row
    rid = lax.broadcasted_iota(jnp.int32, (2 * SUBLANES, l), 0)
    a3, b3 = _pieces3(a_row), _pieces3(bcum_row)
    lhs = jnp.where(rid == 0, a3[0], jnp.where(rid == 1, a3[1], jnp.where(rid == 2, a3[2],
                    jnp.where(rid < 6, 1.0, 0.0))))
    rhs = jnp.where(rid == 3, b3[0], jnp.where(rid == 4, b3[1], jnp.where(rid == 5, b3[2],
                    jnp.where(rid < 3, 1.0, 0.0))))
    s_idx = lax.broadcasted_iota(jnp.int32, (l, l), 0)
    t_idx = lax.broadcasted_iota(jnp.int32, (l, l), 1)
    pair = (s_idx >= t_idx) if rev else (s_idx <= t_idx)
    outer = _dot_tn(lhs.astype(BF16), rhs.astype(BF16))
    yield
    m_st = states[d][2]
    m_prev = m_st[0:1, 0:1]
    log_d = jnp.where(pair, outer, -jnp.inf)
    m_inter = bcum_row + m_prev
    m_t = jnp.maximum(m_inter, jnp.max(log_d, axis=0, keepdims=True))
    dmat = jnp.exp(log_d - m_t)
    w_inter = jnp.exp(m_inter - m_t)
    log_w = g_end + a_row
    m_new = jnp.maximum(g_end + m_prev, jnp.max(log_w, axis=1, keepdims=True))
    wgt = jnp.exp(log_w - m_new)
    decay = jnp.exp(g_end + m_prev - m_new)
    states[d][2] = jnp.broadcast_to(m_new, m_st.shape)
    yield

    c_st, n_st = states[d][0], states[d][1]
    c_b = c_st.astype(BF16)
    q_n = _dot(n_st.astype(BF16), q_t)[0:1, :]
    wgt16 = jnp.where(rid == 0, wgt, 0.0).astype(BF16)
    states[d][0] = decay * c_st + _dot((k_t.astype(F32) * wgt).astype(BF16), vb)
    states[d][1] = decay * n_st + _dot(wgt16, kb)
    yield

    sc = kq * dmat
    den = jnp.sum(sc, axis=0, keepdims=True) + w_inter * q_n
    scale = 1.0 / jnp.maximum(jnp.abs(den), jnp.exp(-m_t))
    lhs_t = jnp.concatenate([(sc * scale).astype(BF16),
                             (q_t.astype(F32) * (w_inter * scale)).astype(BF16)], axis=0)
    hc = _dot_tn(lhs_t, jnp.concatenate([vb, c_b], axis=0))
    if not finalize:
        return hc, kb, q_t, k_t, g_rows
    yield
    prev, og = vals[5:]
    hs = prev + hc
    y = hs * lax.rsqrt(jnp.mean(hs * hs, axis=-1, keepdims=True) + EPS) * norm_g
    return ((y * _sigmoid(og.astype(F32))).astype(BF16),)


def _mlstm_kernel(q_ref, k_ref, v_ref, og_ref, g_ref, cwq_ref, cwk_ref, cbq_ref, cbk_ref, gb_ref, ng_ref,
                  out_ref, h_ref, ks_ref, qt_ref, kt_ref, gt_ref, tri_ref, c_ref, n_ref, m_ref):
    l = ML_CHUNK
    nc = q_ref.shape[0] // l

    def rows_of(c):
        return pl.ds(pl.multiple_of(c * l, l), l)

    c_ref[...] = jnp.zeros_like(c_ref)
    n_ref[...] = jnp.zeros_like(n_ref)
    m_ref[...] = jnp.zeros_like(m_ref)
    tri_ref[0] = _tri3t(l, False)
    tri_ref[1] = _tri3t(l, True)

    def load(c, rev, finalize):
        rows = rows_of(c)
        if finalize:
            return (ks_ref[rows, :], qt_ref[:, rows], kt_ref[:, rows], v_ref[rows, :], gt_ref[:, rows],
                    h_ref[rows, :], og_ref[rows, :])
        return (_load_with_halo(q_ref, c), _load_with_halo(k_ref, c), v_ref[rows, :], g_ref[rows, :],
                c > 0, c < nc - 1)

    def compute(vals, rev, finalize, states):
        params = (cwq_ref[...], cwk_ref[...], cbq_ref[...], cbk_ref[...], gb_ref[...], ng_ref[...])
        return _mlstm_chunk(vals, rev, finalize, states, params, tri_ref[int(rev)])

    def store(c, out, finalize):
        rows = rows_of(c)
        if finalize:
            out_ref[rows, :] = out[0]
        else:
            h_ref[rows, :], ks_ref[rows, :], qt_ref[:, rows], kt_ref[:, rows], gt_ref[:, rows] = out

    _run_scans(nc, ML_UNROLL, load, compute, store,
               [(c_ref.at[d], n_ref.at[d], m_ref.at[d]) for d in (0, 1)])


def _mlstm(p16, p32, conv_w, conv_b, gate_b, norm_g, bsz, seq):
    n = p16.shape[0]
    hd = ML_HEAD_DIM
    blk = lambda off: pl.BlockSpec((seq, hd), lambda b, h, off=off: (b, COL16_ML // hd + off + h))
    wblk = lambda rows, off: pl.BlockSpec((rows, hd), lambda b, h, off=off: (0, off + h))
    return pl.pallas_call(
        _mlstm_kernel,
        grid=(bsz, ML_HEADS),
        in_specs=[
            blk(0), blk(ML_HEADS), blk(2 * ML_HEADS), blk(3 * ML_HEADS),
            pl.BlockSpec((seq, LANES), lambda b, h: (b, COL32_MLG // LANES + h)),
            wblk(3, 0), wblk(3, ML_HEADS), wblk(1, 0), wblk(1, ML_HEADS),
            pl.BlockSpec((1, LANES), lambda b, h: (0, h)),
            wblk(1, 0),
        ],
        out_specs=pl.BlockSpec((seq, hd), lambda b, h: (b, h)),
        out_shape=jax.ShapeDtypeStruct((n, ML_HEADS * hd), BF16),
        scratch_shapes=[pltpu.VMEM((seq, hd), F32),
                        pltpu.VMEM((seq, hd), BF16), pltpu.VMEM((hd, seq), BF16), pltpu.VMEM((hd, seq), BF16),
                        pltpu.VMEM((SUBLANES, seq), F32), pltpu.VMEM((2, 3 * ML_CHUNK, ML_CHUNK), BF16),
                        pltpu.VMEM((2, hd, hd), F32), pltpu.VMEM((2, 2 * SUBLANES, hd), F32),
                        pltpu.VMEM((2, 1, LANES), F32)],
        compiler_params=_params("arbitrary", "arbitrary"),
        name="mlstm",
    )(p16, p16, p16, p16, p32, conv_w, conv_w, conv_b, conv_b, gate_b, norm_g)


def _pieces3(x):
    hi = x.astype(BF16).astype(F32)
    r1 = x - hi
    mid = r1.astype(BF16).astype(F32)
    return hi, mid, r1 - mid


def _split3(x, axis=0):
    return jnp.concatenate([p.astype(BF16) for p in _pieces3(x)], axis=axis)


def _tri3(l, rev):
    t = lax.broadcasted_iota(jnp.int32, (l, 3 * l), 0)
    u = lax.broadcasted_iota(jnp.int32, (l, 3 * l), 1) & (l - 1)
    inside = (u >= t) if rev else (u <= t)
    return jnp.where(inside, 1.0, 0.0).astype(BF16)


def _tri3t(l, rev):
    u = lax.broadcasted_iota(jnp.int32, (3 * l, l), 0) & (l - 1)
    t = lax.broadcasted_iota(jnp.int32, (3 * l, l), 1)
    inside = (u >= t) if rev else (u <= t)
    return jnp.where(inside, 1.0, 0.0).astype(BF16)


def _cumsum_rows(x, tri3):
    return _dot(tri3, _split3(x))


def _level_factors(b, f, q, kk, w, rev, row):
    l = b.shape[0]
    late = (row & w) != 0
    q_side = (~late) if rev else late
    if w == 1:
        return jnp.where(q_side, q * f, kk)
    if w == 2:
        p = row & 3
        f_prev = pltpu.roll(f, 1, axis=0)
        f_next = pltpu.roll(f, l - 1, axis=0)
        if rev:
            e = jnp.where(p == 0, f * f_next, jnp.where(p == 1, f, jnp.where(p == 3, f_prev, 1.0)))
        else:
            e = jnp.where(p == 3, f * f_prev, jnp.where(p == 2, f, jnp.where(p == 0, f_next, 1.0)))
        return jnp.where(q_side, q, kk) * e
    if w == 4:
        pieces = []
        for start in range(0, l, 2 * w):
            r = start + w if rev else start + w - 1
            pieces.append(jnp.broadcast_to(b[r:r + 1, :], (2 * w, b.shape[1])))
        return jnp.where(q_side, q, kk) * jnp.exp2(-jnp.abs(b - jnp.concatenate(pieces, axis=0)))
    base, expo = [], []
    for start in range(0, l, 2 * w):
        lo, hi = slice(start, start + w), slice(start + w, start + 2 * w)
        if rev:
            r = b[start + w:start + w + 1, :]
            base += [q[lo], kk[hi]]
            expo += [b[lo] - r, r - b[hi]]
        else:
            r = b[start + w - 1:start + w, :]
            base += [kk[lo], q[hi]]
            expo += [r - b[lo], b[hi] - r]
    return jnp.concatenate(base, axis=0) * jnp.exp2(jnp.concatenate(expo, axis=0))


def _hgrn_chunk(vals, rev, finalize, states, norm_g, tri3):
    l = HG_CHUNK
    d = int(rev)
    qb, f, vb = vals[:3]
    q = qb.astype(F32)
    kk = 1.0 - f
    b = _cumsum_rows(jnp.log2(f), tri3)
    yield

    st = states[d][0]
    o_inter = _dot_nt((q * jnp.exp2(b)).astype(BF16), st.astype(BF16))
    b_end = b[0:1, :] if rev else b[l - 1:l, :]
    kd = (kk * jnp.exp2(b_end - b)).astype(BF16)
    states[d][0] = jnp.exp2(b_end) * st + _dot_tn(vb, kd)


    row = lax.broadcasted_iota(jnp.int32, (l, HG_DIM), 0)
    t_idx = lax.broadcasted_iota(jnp.int32, (l, l), 0)
    s_idx = lax.broadcasted_iota(jnp.int32, (l, l), 1)
    diff = t_idx ^ s_idx
    att = None
    w = l // 2
    while w >= 1:
        mb = _level_factors(b, f, q, kk, w, rev, row).astype(BF16)
        a = _dot_nt(mb, mb)
        att = a if att is None else jnp.where(diff < 2 * w, a, att)
        w //= 2
    yield

    keep = (s_idx > t_idx) if rev else (s_idx < t_idx)
    att = jnp.where(diff == 0, jnp.sum(q * kk, axis=1, keepdims=True), jnp.where(keep, att, 0.0))
    o = _dot(att.astype(BF16), vb) + o_inter
    if finalize:
        yield
        prev, gg = vals[3:]
        gg = gg.astype(F32)
        hs = prev + o
        y = hs * lax.rsqrt(jnp.mean(hs * hs, axis=-1, keepdims=True) + EPS) * norm_g
        o = (y * (gg * _sigmoid(gg))).astype(BF16)
    return o


def _hgrn_kernel(q_ref, v_ref, gg_ref, ff_ref, fb_ref, ng_ref, out_ref, h_ref, st_ref, tri_ref):
    l = HG_CHUNK
    nc = q_ref.shape[0] // l
    st_ref[...] = jnp.zeros_like(st_ref)
    tri_ref[0] = _tri3(l, False)
    tri_ref[1] = _tri3(l, True)

    def rows_of(c):
        return pl.ds(pl.multiple_of(c * l, l), l)

    def load(c, rev, finalize):
        rows = rows_of(c)
        vals = (q_ref[rows, :], (fb_ref if rev else ff_ref)[rows, :], v_ref[rows, :])
        if finalize:
            vals += (h_ref[rows, :], gg_ref[rows, :])
        return vals

    def compute(vals, rev, finalize, states):
        return _hgrn_chunk(vals, rev, finalize, states, ng_ref[...], tri_ref[int(rev)])

    def store(c, out, finalize):
        (out_ref if finalize else h_ref)[rows_of(c), :] = out

    _run_scans(nc, HG_UNROLL, load, compute, store, [(st_ref.at[0],), (st_ref.at[1],)])


def _hgrn(p16, p32, norm_g, bsz, seq):
    n = p16.shape[0]
    blk16 = lambda grp: pl.BlockSpec(
        (seq, HG_DIM), lambda b, h, grp=grp: (b, COL16_HG // HG_DIM + grp * HG_HEADS + h))
    blk32 = lambda grp: pl.BlockSpec(
        (seq, HG_DIM), lambda b, h, grp=grp: (b, COL32_HGF // HG_DIM + grp * HG_HEADS + h))
    return pl.pallas_call(
        _hgrn_kernel,
        grid=(bsz, HG_HEADS),
        in_specs=[
            blk16(0), blk16(1), blk16(2), blk32(0), blk32(1),
            pl.BlockSpec((1, HG_DIM), lambda b, h: (0, h)),
        ],
        out_specs=pl.BlockSpec((seq, HG_DIM), lambda b, h: (b, h)),
        out_shape=jax.ShapeDtypeStruct((n, HG_HEADS * HG_DIM), BF16),
        scratch_shapes=[pltpu.VMEM((seq, HG_DIM), F32), pltpu.VMEM((2, HG_DIM, HG_DIM), F32),
                        pltpu.VMEM((2, HG_CHUNK, 3 * HG_CHUNK), BF16)],
        compiler_params=_params("arbitrary", "arbitrary"),
        name="hgrn2",
    )(p16, p16, p16, p32, p32, norm_g)


def _merge_kernel(x_ref, ym_ref, yh_ref, gm_ref, gh_ref, wm_ref, wh_ref, wo_ref, o_ref):
    pm = _dot(ym_ref[...], wm_ref[...])
    ph = _dot(yh_ref[...], wh_ref[...])
    merged = _sigmoid(gm_ref[...].astype(F32)) * pm + _sigmoid(gh_ref[...].astype(F32)) * ph
    o_ref[...] = x_ref[...] + _dot(merged.astype(BF16), wo_ref[...])


def _merge(x, y_m, y_h, p16, w_m, w_h, w_o, tm):
    n, d = x.shape
    tm = min(tm, n)
    row = lambda cb: pl.BlockSpec((tm, d), lambda i, cb=cb: (i, cb))
    full = pl.BlockSpec((d, d), lambda i: (0, 0))
    return pl.pallas_call(
        _merge_kernel,
        grid=(n // tm,),
        in_specs=[row(0), row(0), row(0), row(COL16_GATE // d), row(COL16_GATE // d + 1), full, full, full],
        out_specs=row(0),
        out_shape=jax.ShapeDtypeStruct((n, d), F32),
        compiler_params=_params("arbitrary"),
        name="merge",
    )(x, y_m, y_h, p16, p16, w_m, w_h, w_o)


def _xattn_kernel(x_ref, g_ref, wq_ref, kv_ref, wo_ref, o_ref):
    x = x_ref[...]
    u = _rms(x, g_ref[...]).astype(BF16)
    q = _dot(u, wq_ref[...])
    heads = []
    for h in range(X_HEADS):
        lo = h * X_HEAD_DIM
        qh = q[:, lo:lo + X_HEAD_DIM].astype(BF16)
        kh = kv_ref[:, lo:lo + X_HEAD_DIM]
        vh = kv_ref[:, D_MODEL + lo:D_MODEL + lo + X_HEAD_DIM]
        s = _dot_nt(qh, kh) * (X_HEAD_DIM ** -0.5)
        p = jnp.exp(s - jnp.max(s, axis=-1, keepdims=True))
        p = p * (1.0 / jnp.sum(p, axis=-1, keepdims=True))
        heads.append(_dot(p.astype(BF16), vh).astype(BF16))
    att = jnp.concatenate(heads, axis=-1)
    o_ref[...] = x + _dot(att, wo_ref[...])


def _xattn(x, g, w_q, kv, w_o, seq, n_mem, tm):
    n, d = x.shape
    tm = min(tm, seq)
    per_seq = seq // tm
    full = pl.BlockSpec((d, d), lambda i: (0, 0))
    return pl.pallas_call(
        _xattn_kernel,
        grid=(n // tm,),
        in_specs=[
            pl.BlockSpec((tm, d), lambda i: (i, 0)),
            pl.BlockSpec((1, d), lambda i: (0, 0)),
            full,
            pl.BlockSpec((n_mem, 2 * d), lambda i: (i // per_seq, 0)),
            full,
        ],
        out_specs=pl.BlockSpec((tm, d), lambda i: (i, 0)),
        out_shape=jax.ShapeDtypeStruct((n, d), F32),
        compiler_params=_params("arbitrary"),
        name="xattn",
    )(x, g.reshape(1, d), w_q, kv, w_o)


def _gelu(x):
    return 0.5 * x * (1.0 + jnp.tanh(0.7978845608028654 * (x + 0.044715 * (x * x * x))))


def _ffn_kernel(x_ref, xp_ref, xn_ref, g_ref, wup_ref, cw_ref, cb_ref, wdn_ref, gf_ref, o_ref, *,
                per_seq, final_norm):
    i = pl.program_id(0)
    tm = x_ref.shape[0]
    x = x_ref[...]
    g = g_ref[...]
    first = (i % per_seq) == 0
    last = (i % per_seq) == per_seq - 1
    up_prev = jnp.where(first, 0.0, _rms(xp_ref[...], g))
    up_next = jnp.where(last, 0.0, _rms(xn_ref[...], g))
    u_ext = jnp.concatenate([up_prev, _rms(x, g), up_next], axis=0).astype(BF16)
    ext = tm + 2 * SUBLANES
    acc = x
    for c0 in range(0, D_FF, FF_CHUNK):
        a_ext = _dot(u_ext, wup_ref[:, c0:c0 + FF_CHUNK])
        gate = _dot(u_ext, wup_ref[:, D_FF + c0:D_FF + c0 + FF_CHUNK])[SUBLANES:SUBLANES + tm, :]
        a_prev = pltpu.roll(a_ext, 1, axis=0)[SUBLANES:SUBLANES + tm, :]
        a_next = pltpu.roll(a_ext, ext - 1, axis=0)[SUBLANES:SUBLANES + tm, :]
        a_cur = a_ext[SUBLANES:SUBLANES + tm, :]
        a = (cw_ref[0:1, c0:c0 + FF_CHUNK] * a_prev + cw_ref[1:2, c0:c0 + FF_CHUNK] * a_cur
             + cw_ref[2:3, c0:c0 + FF_CHUNK] * a_next + cb_ref[:, c0:c0 + FF_CHUNK])
        acc = acc + _dot((_gelu(a) * gate).astype(BF16), wdn_ref[c0:c0 + FF_CHUNK, :])
    o_ref[...] = _rms(acc, gf_ref[...]) if final_norm else acc


def _ffn(x, g, w_up, conv_w, conv_b, w_dn, g_final, final_norm, seq, tm):
    n, d = x.shape
    tm = min(tm, seq)
    per_seq = seq // tm
    tb = tm // SUBLANES
    nb = n // SUBLANES
    const = lambda shape: pl.BlockSpec(shape, lambda i: (0, 0))
    return pl.pallas_call(
        functools.partial(_ffn_kernel, per_seq=per_seq, final_norm=final_norm),
        grid=(n // tm,),
        in_specs=[
            pl.BlockSpec((tm, d), lambda i: (i, 0)),
            pl.BlockSpec((SUBLANES, d), lambda i: (jnp.maximum(i * tb - 1, 0), 0)),
            pl.BlockSpec((SUBLANES, d), lambda i: (jnp.minimum((i + 1) * tb, nb - 1), 0)),
            const((1, d)),
            const((d, 2 * D_FF)),
            const((3, D_FF)),
            const((1, D_FF)),
            const((D_FF, d)),
            const((1, d)),
        ],
        out_specs=pl.BlockSpec((tm, d), lambda i: (i, 0)),
        out_shape=jax.ShapeDtypeStruct((n, d), F32),
        compiler_params=_params("arbitrary"),
        name="ffn",
    )(x, x, x, g.reshape(1, d), w_up, conv_w, conv_b.reshape(1, D_FF), w_dn, g_final.reshape(1, d))


def kernel(x, mem, norm_mix_g, w_in, ml_conv_w, ml_conv_b, ml_gate_b, ml_norm_g, hg_lb, hg_norm_g,
           w_branch_m, w_branch_h, w_out, norm_x_g, norm_mem_g, w_q_x, w_kv_x, w_o_x, norm_ffn_g,
           w_up, ffn_conv_w, ffn_conv_b, w_down, norm_final_g):
    bsz, seq, d = x.shape
    n_mem = mem.shape[1]
    depth = w_in.shape[0]
    n = bsz * seq
    xf = x.reshape(n, d)
    memf = mem.reshape(bsz * n_mem, d)
    lb_all = jnp.cumsum(jax.nn.softmax(hg_lb.astype(F32), axis=1), axis=1)

    for l in range(depth):
        w = w_in[l].astype(BF16)
        wg = w[:, 4096:4096 + 4 * ML_HEADS].reshape(d, 4, ML_HEADS).transpose(0, 2, 1)
        wg = jnp.pad(wg, ((0, 0), (0, 0), (0, LANES - 4))).reshape(d, ML_HEADS * LANES)
        gate_b = jnp.pad(ml_gate_b[l].T, ((0, 0), (0, LANES - 4))).reshape(1, ML_HEADS * LANES)
        o = 4096 + 4 * ML_HEADS
        hg_q, hg_ff, hg_fb, hg_i, hg_g, gates = (w[:, o:o + 1024], w[:, o + 1024:o + 2048],
                                                 w[:, o + 2048:o + 3072], w[:, o + 3072:o + 4096],
                                                 w[:, o + 4096:o + 5120], w[:, o + 5120:])
        w_proj = jnp.concatenate([w[:, :4096], hg_q, hg_i, hg_g, gates, hg_ff, hg_fb, wg], axis=1)
        lb_row = jnp.pad(lb_all[:, l].reshape(1, 2 * D_MODEL), ((0, 0), (0, ML_HEADS * LANES)))

        p16, p32 = _in_proj(xf, norm_mix_g[l], w_proj, lb_row, COLS16, COLS32,
                            (COL16_HG, COL16_HG + 1024), (COLS16 + COL32_HGF, COLS16 + COL32_MLG),
                            TM_INPROJ, TN_INPROJ)
        y_m = _mlstm(p16, p32, ml_conv_w[l], ml_conv_b[l].reshape(1, -1), gate_b,
                     ml_norm_g[l].reshape(1, -1), bsz, seq)
        y_h = _hgrn(p16, p32, hg_norm_g[l].reshape(1, -1), bsz, seq)
        xf = _merge(xf, y_m, y_h, p16, w_branch_m[l].astype(BF16), w_branch_h[l].astype(BF16),
                    w_out[l].astype(BF16), TM_MERGE)

        kv = _norm_matmul(memf, norm_mem_g[l], w_kv_x[l].astype(BF16), BF16, TM_PROJ, TN_PROJ)
        xf = _xattn(xf, norm_x_g[l], w_q_x[l].astype(BF16), kv, w_o_x[l].astype(BF16), seq, n_mem, TM_XATTN)

        xf = _ffn(xf, norm_ffn_g[l], w_up[l].astype(BF16), ffn_conv_w[l], ffn_conv_b[l],
                  w_down[l].astype(BF16), norm_final_g, l == depth - 1, seq, TM_FFN)
    return xf.reshape(bsz, seq, d)
```
